```python
import math
import jax, jax.numpy as jnp
from jax import lax
import numpy as np

D_MODEL = 1024
BATCH = 1
SEQ = 16384
DEPTH = 2

PLE_DIM = 256
N_MIXERS = 4
GROUP_W = 256
MIX_W = N_MIXERS * GROUP_W
EPS = 1e-6
CONV_K = 4
RET_HEADS = 4
RET_HD = GROUP_W // RET_HEADS
RET_CHUNK = 128
ROPE_BASE = 10000.0
LRU_BLOCKS = 4
LRU_BW = GROUP_W // LRU_BLOCKS
LRU_C = 8.0
S5_GW = 16
S5_GROUPS = GROUP_W // S5_GW
S5_STATE = 64
GDN_HEADS = 4
GDN_HD = GROUP_W // GDN_HEADS
GDN_CHUNK = 64
PEER_HEADS = 8
PEER_NKEYS = 128
PEER_N = PEER_NKEYS * PEER_NKEYS
PEER_DKEY = 128
PEER_HALF = PEER_DKEY // 2
PEER_TOPK = 16
PEER_BLOCK = 128

IN_SIZES = (GROUP_W, GROUP_W, GROUP_W, GROUP_W,
            GROUP_W, GROUP_W,
            GROUP_W,
            3 * GROUP_W, GDN_HEADS, GDN_HEADS, GROUP_W)
IN_W = sum(IN_SIZES)
IN_SPLITS = [int(c) for c in np.cumsum(IN_SIZES)[:-1]]

kernel_name = 'hybrid_retention_rglru_s5_gdn_peer'


def rmsnorm(x, g):
    xf = x.astype(jnp.float32)
    y = xf * lax.rsqrt(jnp.mean(xf * xf, axis=-1, keepdims=True) + EPS)
    return (y * g.astype(jnp.float32)).astype(x.dtype)


def causal_dwconv(x, w):
    s = x.shape[1]
    xp = jnp.pad(x, ((0, 0), (CONV_K - 1, 0), (0, 0)))
    return sum(w[j] * xp[:, j:j + s] for j in range(CONV_K))


def rotary(t, positions):
    half = t.shape[-1] // 2
    inv_freq = ROPE_BASE ** (-jnp.arange(half, dtype=jnp.float32) / half)
    ang = positions.astype(jnp.float32)[..., None] * inv_freq
    cos = jnp.cos(ang)[:, :, None, :]
    sin = jnp.sin(ang)[:, :, None, :]
    t1, t2 = t[..., :half], t[..., half:]
    return jnp.concatenate([t1 * cos - t2 * sin, t2 * cos + t1 * sin], axis=-1)


def linear_scan(a, b):
    def combine(e1, e2):
        return e1[0] * e2[0], e2[0] * e1[1] + e2[1]
    return lax.associative_scan(combine, (a, b), axis=1)[1]


def retention(q, k, v, g, gn_w, positions):
    bsz, s, _ = q.shape
    n = s // RET_CHUNK
    shp = (bsz, s, RET_HEADS, RET_HD)
    q = rotary(q.astype(jnp.float32).reshape(shp), positions) * RET_HD ** -0.5
    k = rotary(k.astype(jnp.float32).reshape(shp), positions)
    v = v.astype(jnp.float32).reshape(shp)
    to_chunks = lambda t: t.reshape(bsz, n, RET_CHUNK, RET_HEADS, RET_HD).transpose(1, 0, 3, 2, 4)
    log_gamma = jnp.log(1.0 - 2.0 ** (-5.0 - jnp.arange(RET_HEADS, dtype=jnp.float32)))
    idx = jnp.arange(RET_CHUNK, dtype=jnp.float32)
    diff = idx[:, None] - idx[None, :]
    causal = diff >= 0
    decay_mask = jnp.where(causal, jnp.exp(log_gamma[:, None, None] * jnp.where(causal, diff, 0.0)), 0.0)
    q_decay = jnp.exp(log_gamma[:, None] * (idx + 1.0))[None, :, :, None]
    k_decay = jnp.exp(log_gamma[:, None] * (RET_CHUNK - 1.0 - idx))[None, :, :, None]
    chunk_decay = jnp.exp(log_gamma * RET_CHUNK)[None, :, None, None]

    def step(state, inp):
        qc, kc, vc = inp
        scores = jnp.einsum('bhid,bhjd->bhij', qc, kc) * decay_mask
        out = (jnp.einsum('bhij,bhjd->bhid', scores, vc)
               + jnp.einsum('bhid,bhde->bhie', qc, state) * q_decay)
        state = state * chunk_decay + jnp.einsum('bhjd,bhje->bhde', kc * k_decay, vc)
        return state, out

    state0 = jnp.zeros((bsz, RET_HEADS, RET_HD, RET_HD), jnp.float32)
    _, o = lax.scan(step, state0, (to_chunks(q), to_chunks(k), to_chunks(v)))
    o = o.transpose(1, 0, 3, 2, 4).reshape(shp)
    mu = jnp.mean(o, axis=-1, keepdims=True)
    var = jnp.mean(jnp.square(o - mu), axis=-1, keepdims=True)
    o = ((o - mu) * lax.rsqrt(var + EPS)).reshape(bsz, s, GROUP_W) * gn_w.astype(jnp.float32)
    return (jax.nn.silu(g.astype(jnp.float32)) * o).astype(g.dtype)


def rglru(gate_in, x_in, conv_w, conv_b, w_a, b_a, w_x, b_x, lam):
    bsz, s, _ = x_in.shape
    f32 = jnp.float32
    xb = (causal_dwconv(x_in, conv_w) + conv_b).astype(f32)
    xblk = xb.reshape(bsz, s, LRU_BLOCKS, LRU_BW)
    r = jax.nn.sigmoid(jnp.einsum('bsnc,ncd->bsnd', xblk, w_a.astype(f32)).reshape(bsz, s, GROUP_W) + b_a.astype(f32))
    i = jax.nn.sigmoid(jnp.einsum('bsnc,ncd->bsnd', xblk, w_x.astype(f32)).reshape(bsz, s, GROUP_W) + b_x.astype(f32))
    log_a = -LRU_C * r * jax.nn.softplus(-lam.astype(f32))
    a = jnp.exp(log_a)
    b = jnp.sqrt(-jnp.expm1(2.0 * log_a)) * (i * xb)
    h = linear_scan(a, b)
    return (jax.nn.gelu(gate_in.astype(f32)) * h).astype(x_in.dtype)


def s5(u, a_re, a_im, b_re, b_im, c_re, c_im, d, log_dt, glu_w, glu_b):
    bsz, s, _ = u.shape
    f32 = jnp.float32
    uf = u.astype(f32)
    a_re, a_im = a_re.astype(f32), a_im.astype(f32)
    dt = jnp.exp(log_dt.astype(f32))[:, None]
    mag = jnp.exp(a_re * dt)
    ang = a_im * dt
    ab_re, ab_im = mag * jnp.cos(ang), mag * jnp.sin(ang)
    den = a_re * a_re + a_im * a_im
    p_re, p_im = ab_re - 1.0, ab_im
    f_re = (p_re * a_re + p_im * a_im) / den
    f_im = (p_im * a_re - p_re * a_im) / den
    b_re, b_im = b_re.astype(f32), b_im.astype(f32)
    bb_re = f_re[..., None] * b_re - f_im[..., None] * b_im
    bb_im = f_re[..., None] * b_im + f_im[..., None] * b_re
    ug = uf.reshape(bsz, s, S5_GROUPS, S5_GW)
    bu_re = jnp.einsum('bsgc,gnc->bsgn', ug, bb_re)
    bu_im = jnp.einsum('bsgc,gnc->bsgn', ug, bb_im)
    at_re = jnp.broadcast_to(ab_re, bu_re.shape)
    at_im = jnp.broadcast_to(ab_im, bu_im.shape)

    def combine(e1, e2):
        ar1, ai1, br1, bi1 = e1
        ar2, ai2, br2, bi2 = e2
        return (ar2 * ar1 - ai2 * ai1, ar2 * ai1 + ai2 * ar1,
                ar2 * br1 - ai2 * bi1 + br2, ar2 * bi1 + ai2 * br1 + bi2)

    _, _, x_re, x_im = lax.associative_scan(combine, (at_re, at_im, bu_re, bu_im), axis=1)
    y = (jnp.einsum('bsgn,gcn->bsgc', x_re, c_re.astype(f32))
         - jnp.einsum('bsgn,gcn->bsgc', x_im, c_im.astype(f32)))
    y = y.reshape(bsz, s, GROUP_W) + d.astype(f32) * uf
    zz = jax.nn.gelu(y) @ glu_w.astype(f32) + glu_b.astype(f32)
    out = zz[..., :GROUP_W] * jax.nn.sigmoid(zz[..., GROUP_W:])
    return out.astype(u.dtype)


def gated_deltanet(qkv, b_logit, a_logit, g, conv_w, a_log, dt_bias, norm_w):
    bsz, s, _ = qkv.shape
    n = s // GDN_CHUNK
    f32 = jnp.float32
    qkv = jax.nn.silu(causal_dwconv(qkv, conv_w).astype(f32)).reshape(bsz, s, 3, GDN_HEADS, GDN_HD)
    l2 = lambda t: t * lax.rsqrt(jnp.sum(t * t, axis=-1, keepdims=True) + EPS)
    q = l2(qkv[:, :, 0]) * GDN_HD ** -0.5
    k = l2(qkv[:, :, 1])
    v = qkv[:, :, 2]
    beta = jax.nn.sigmoid(b_logit.astype(f32))
    log_alpha = -jnp.exp(a_log.astype(f32)) * jax.nn.softplus(a_logit.astype(f32) + dt_bias.astype(f32))
    chunk = lambda t: jnp.moveaxis(t.reshape(bsz, n, GDN_CHUNK, GDN_HEADS, *t.shape[3:]), 3, 1)
    qc, kc, vc = chunk(q), chunk(k), chunk(v)
    bc = chunk(beta)
    gcum = jnp.cumsum(chunk(log_alpha), axis=-1)
    idx = jnp.arange(GDN_CHUNK)
    incl = idx[:, None] >= idx[None, :]
    strict = idx[:, None] > idx[None, :]
    diff = gcum[..., :, None] - gcum[..., None, :]
    lmask = jnp.where(incl, jnp.exp(jnp.where(incl, diff, 0.0)), 0.0)
    kb = kc * bc[..., None]
    lower = jnp.where(strict, jnp.einsum('bhnid,bhnjd->bhnij', kb, kc) * lmask, 0.0)
    eye = jnp.eye(GDN_CHUNK, dtype=f32)
    rhs = jnp.concatenate([vc * bc[..., None], kb * jnp.exp(gcum)[..., None]], axis=-1)
    sol = lax.linalg.triangular_solve(eye + lower, rhs, left_side=True, lower=True)
    u_c, w_c = sol[..., :GDN_HD], sol[..., GDN_HD:]
    attn = jnp.where(incl, jnp.einsum('bhnid,bhnjd->bhnij', qc, kc) * lmask, 0.0)
    q_dec = qc * jnp.exp(gcum)[..., None]
    k_dec = kc * jnp.exp(gcum[..., -1:] - gcum)[..., None]
    c_dec = jnp.exp(gcum[..., -1])[..., None, None]

    def step(state, inp):
        a_i, u_i, w_i, qd_i, kd_i, cd_i = inp
        v_new = u_i - jnp.einsum('bhcd,bhde->bhce', w_i, state)
        out = jnp.einsum('bhcd,bhde->bhce', qd_i, state) + jnp.einsum('bhij,bhje->bhie', a_i, v_new)
        state = state * cd_i + jnp.einsum('bhcd,bhce->bhde', kd_i, v_new)
        return state, out

    xs = tuple(jnp.moveaxis(t, 2, 0) for t in (attn, u_c, w_c, q_dec, k_dec, c_dec))
    state0 = jnp.zeros((bsz, GDN_HEADS, GDN_HD, GDN_HD), f32)
    _, o = lax.scan(step, state0, xs)
    o = o.transpose(1, 0, 3, 2, 4).reshape(bsz, s, GDN_HEADS, GDN_HD)
    o = rmsnorm(o, norm_w)
    gate = jax.nn.silu(g.astype(f32)).reshape(bsz, s, GDN_HEADS, GDN_HD)
    return (o * gate).reshape(bsz, s, GROUP_W).astype(g.dtype)


def peer(z, w_q, sub_keys, u_tab, v_tab):
    bsz, s, d = z.shape
    q = (z @ w_q).reshape(bsz, s, PEER_HEADS, 2, PEER_HALF)
    sc = jnp.einsum('bshpd,hpkd->bshpk', q, sub_keys).astype(jnp.float32)
    s1, i1 = lax.top_k(sc[..., 0, :], PEER_TOPK)
    s2, i2 = lax.top_k(sc[..., 1, :], PEER_TOPK)
    n_cand = PEER_TOPK * PEER_TOPK
    cand_s = (s1[..., :, None] + s2[..., None, :]).reshape(bsz, s, PEER_HEADS, n_cand)
    cand_id = (i1[..., :, None] * PEER_NKEYS + i2[..., None, :]).reshape(bsz, s, PEER_HEADS, n_cand)
    top_s, top_pos = lax.top_k(cand_s, PEER_TOPK)
    expert_id = jnp.take_along_axis(cand_id, top_pos, axis=-1)
    gates = jax.nn.softmax(top_s, axis=-1).astype(z.dtype)
    n_sel = PEER_HEADS * PEER_TOPK
    nblk = bsz * s // PEER_BLOCK
    zb = z.reshape(nblk, PEER_BLOCK, d)
    eb = expert_id.reshape(nblk, PEER_BLOCK, n_sel)
    gb = gates.reshape(nblk, PEER_BLOCK, n_sel)

    def block(args):
        zt, et, gt = args
        act = jax.nn.gelu(jnp.einsum('tkd,td->tk', u_tab[et], zt))
        return jnp.einsum('tk,tkd->td', gt * act, v_tab[et])

    return lax.map(block, (zb, eb, gb)).reshape(bsz, s, d)


def setup_inputs(seed: int = 0) -> dict:
    key = jax.random.key(seed)
    ks = iter(jax.random.split(key, 48))
    f32 = jnp.float32
    L = DEPTH

    def nrm(shape, scale):
        return jax.random.normal(next(ks), shape, f32) * scale

    def gain(shape):
        return 1.0 + nrm(shape, 0.02)

    def unif(shape, lo, hi):
        return jax.random.uniform(next(ks), shape, f32, lo, hi)

    lru_a = unif((L, GROUP_W), 0.9, 0.999) ** (1.0 / LRU_C)
    gdn_dt = jnp.exp(unif((L, GDN_HEADS), math.log(1e-3), math.log(1e-1)))
    return {
        'x': nrm((BATCH, SEQ, D_MODEL), 1.0),
        'p': nrm((DEPTH, BATCH, SEQ, PLE_DIM), 1.0),
        'positions': jnp.broadcast_to(jnp.arange(SEQ, dtype=jnp.int32)[None, :], (BATCH, SEQ)),
        'mix_norm': gain((L, D_MODEL)),
        'w_in': nrm((L, D_MODEL, IN_W), D_MODEL ** -0.5),
        'ret_gn': gain((L, GROUP_W)),
        'lru_conv_w': nrm((L, CONV_K, GROUP_W), CONV_K ** -0.5),
        'lru_conv_b': nrm((L, GROUP_W), 0.01),
        'lru_w_a': nrm((L, LRU_BLOCKS, LRU_BW, LRU_BW), LRU_BW ** -0.5),
        'lru_b_a': nrm((L, GROUP_W), 0.01),
        'lru_w_x': nrm((L, LRU_BLOCKS, LRU_BW, LRU_BW), LRU_BW ** -0.5),
        'lru_b_x': nrm((L, GROUP_W), 0.01),
        'lru_lambda': jnp.log(lru_a) - jnp.log1p(-lru_a),
        's5_a_re': -0.5 + nrm((L, S5_GROUPS, S5_STATE), 0.01),
        's5_a_im': jnp.pi * jnp.arange(S5_STATE, dtype=f32)[None, None, :] + nrm((L, S5_GROUPS, S5_STATE), 0.01),
        's5_b_re': nrm((L, S5_GROUPS, S5_STATE, S5_GW), (2.0 * S5_GW) ** -0.5),
        's5_b_im': nrm((L, S5_GROUPS, S5_STATE, S5_GW), (2.0 * S5_GW) ** -0.5),
        's5_c_re': nrm((L, S5_GROUPS, S5_GW, S5_STATE), S5_STATE ** -0.5),
        's5_c_im': nrm((L, S5_GROUPS, S5_GW, S5_STATE), S5_STATE ** -0.5),
        's5_d': nrm((L, GROUP_W), 1.0),
        's5_log_dt': unif((L, S5_GROUPS), math.log(1e-3), math.log(1e-1)),
        's5_glu_w': nrm((L, GROUP_W, 2 * GROUP_W), GROUP_W ** -0.5),
        's5_glu_b': nrm((L, 2 * GROUP_W), 0.01),
        'gdn_conv_w': nrm((L, CONV_K, 3 * GROUP_W), CONV_K ** -0.5),
        'gdn_a_log': jnp.log(unif((L, GDN_HEADS), 1.0, 16.0)),
        'gdn_dt_bias': gdn_dt + jnp.log(-jnp.expm1(-gdn_dt)),
        'gdn_norm': gain((L, GDN_HD)),
        'branch_norm': gain((L, N_MIXERS, GROUP_W)),
        'w_out': nrm((L, MIX_W, D_MODEL), MIX_W ** -0.5),
        'ffn_norm': gain((L, D_MODEL)),
        'peer_wq': nrm((L, D_MODEL, PEER_HEADS * PEER_DKEY), D_MODEL ** -0.5),
        'peer_subkeys': nrm((L, PEER_HEADS, 2, PEER_NKEYS, PEER_HALF), PEER_HALF ** -0.5),
        'peer_u': nrm((L, PEER_N, D_MODEL), D_MODEL ** -0.5),
        'peer_v': nrm((L, PEER_N, D_MODEL), (PEER_HEADS * PEER_TOPK) ** -0.5),
        'ple_norm': gain((L, D_MODEL)),
        'ple_wg': nrm((L, D_MODEL, D_MODEL), D_MODEL ** -0.5),
        'ple_wp': nrm((L, PLE_DIM, D_MODEL), PLE_DIM ** -0.5),
        'final_norm': gain((D_MODEL,)),
    }


def reference(x, p, positions, mix_norm, w_in, ret_gn, lru_conv_w, lru_conv_b, lru_w_a, lru_b_a,
              lru_w_x, lru_b_x, lru_lambda, s5_a_re, s5_a_im, s5_b_re, s5_b_im, s5_c_re, s5_c_im,
              s5_d, s5_log_dt, s5_glu_w, s5_glu_b, gdn_conv_w, gdn_a_log, gdn_dt_bias, gdn_norm,
              branch_norm, w_out, ffn_norm, peer_wq, peer_subkeys, peer_u, peer_v, ple_norm,
              ple_wg, ple_wp, final_norm):
    h = x
    for l in range(DEPTH):
        z = rmsnorm(h, mix_norm[l])
        (rq, rk, rv, rg, lg, lx, su, dqkv, db, da, dg) = jnp.split(z @ w_in[l], IN_SPLITS, axis=-1)
        y_ret = retention(rq, rk, rv, rg, ret_gn[l], positions)
        y_lru = rglru(lg, lx, lru_conv_w[l], lru_conv_b[l], lru_w_a[l], lru_b_a[l],
                      lru_w_x[l], lru_b_x[l], lru_lambda[l])
        y_s5 = s5(su, s5_a_re[l], s5_a_im[l], s5_b_re[l], s5_b_im[l], s5_c_re[l], s5_c_im[l],
                  s5_d[l], s5_log_dt[l], s5_glu_w[l], s5_glu_b[l])
        y_gdn = gated_deltanet(dqkv, db, da, dg, gdn_conv_w[l], gdn_a_log[l], gdn_dt_bias[l], gdn_norm[l])
        groups = (y_ret, y_lru, y_s5, y_gdn)
        mixed = jnp.concatenate([rmsnorm(y, branch_norm[l, j]) for j, y in enumerate(groups)], axis=-1)
        h = h + mixed @ w_out[l]
        h = h + peer(rmsnorm(h, ffn_norm[l]), peer_wq[l], peer_subkeys[l], peer_u[l], peer_v[l])
        gate = jax.nn.sigmoid(rmsnorm(h, ple_norm[l]) @ ple_wg[l])
        h = h + (p[l] @ ple_wp[l]) * gate
    return rmsnorm(h, final_norm)
```

```python
import functools
import math

import jax
import jax.numpy as jnp
import numpy as np
from jax import lax
from jax.experimental import pallas as pl
from jax.experimental.pallas import tpu as pltpu

f32 = jnp.float32
bf16 = jnp.bfloat16
_HI = lax.Precision.HIGHEST
_NT = (((1,), (1,)), ((), ()))
_TN = (((0,), (0,)), ((), ()))

EPS = 1e-6
D_MODEL = 1024
GROUP_W = 256
CONV_K = 4
RET_HEADS = 4
RET_HD = 64
RET_CHUNK = 128
ROPE_BASE = 10000.0
LRU_C = 8.0
S5_GROUPS = 16
S5_GW = 16
S5_STATE = 64
S5_N = S5_GROUPS * S5_STATE
GDN_HEADS = 4
GDN_HD = 64
GDN_CHUNK = 64
PEER_HEADS = 8
PEER_NKEYS = 128
PEER_HALF = 64
PEER_TOPK = 16
PEER_NCAND = 64

IN_PAD = 2944
VMEM_LIMIT = 48 * 1024 * 1024

TB_IN = 256
TB_RET = 512
TB_LRU = 512
TB_S5 = 512
TB_GDN = 256
TB_OUT = 512
TB_TOPK = 256
TB_PEER = 512
NE_TILE = 512
TB_PLE = 512


def _dot(a, b, dims=(((1,), (0,)), ((), ())), hi=False):
    if hi:
        return lax.dot_general(a, b, dims, precision=_HI, preferred_element_type=f32)
    return lax.dot_general(a.astype(bf16), b.astype(bf16), dims, preferred_element_type=f32)


def _rms(x, w):
    return x * lax.rsqrt(jnp.mean(x * x, axis=-1, keepdims=True) + EPS) * w


def _shift_rows(x, d, fill):
    row = lax.broadcasted_iota(jnp.int32, x.shape, 0)
    return jnp.where(row >= d, pltpu.roll(x, d, 0), fill)


def _params(n_axes=1):
    return pltpu.CompilerParams(dimension_semantics=("arbitrary",) * n_axes, vmem_limit_bytes=VMEM_LIMIT)


def _full(shape):
    n = len(shape)
    return pl.BlockSpec(shape, lambda *_: (0,) * n)


def _rows(tb, w):
    return pl.BlockSpec((tb, w), lambda i: (i, 0))


def _rope_kernel(pos_ref, freq_ref, cos_ref, sin_ref):
    ang = pos_ref[...] * freq_ref[...]
    cos_ref[...] = jnp.cos(ang)
    sin_ref[...] = jnp.sin(ang)


def _rope_tables(pos, freq):
    s = pos.shape[0]
    tb = min(s, 1024)
    return pl.pallas_call(
        _rope_kernel, grid=(s // tb,),
        in_specs=[_rows(tb, 1), _full((1, 128))],
        out_specs=[_rows(tb, 128), _rows(tb, 128)],
        out_shape=[jax.ShapeDtypeStruct((s, 128), f32)] * 2,
        compiler_params=_params(), name="rope_tables")(pos, freq)


def _in_kernel(h_ref, nw_ref, w_ref, ret_ref, lru_ref, s5_ref, gdn_ref, ba_ref):
    z = _rms(h_ref[...], nw_ref[...]).astype(bf16)
    off = 0
    for ref in (ret_ref, lru_ref, s5_ref, gdn_ref, ba_ref):
        w = ref.shape[1]
        ref[...] = jnp.dot(z, w_ref[:, off:off + w], preferred_element_type=f32)
        off += w


def _in_proj(h, nw, w):
    s = h.shape[0]
    tb = min(s, TB_IN)
    widths = (1024, 512, 256, 1024, 128)
    return pl.pallas_call(
        _in_kernel, grid=(s // tb,),
        in_specs=[_rows(tb, D_MODEL), _full((1, D_MODEL)), _full((D_MODEL, IN_PAD))],
        out_specs=[_rows(tb, w_) for w_ in widths],
        out_shape=[jax.ShapeDtypeStruct((s, w_), f32) for w_ in widths],
        compiler_params=_params(), name="in_proj")(h, nw, w)


def _ret_kernel(x_ref, cos_ref, sin_ref, dmask_ref, qdec_ref, kdec_ref, cdec_ref, bmask_ref, bones_ref,
                gn_ref, bn_ref, out_ref, state_ref, o_scr):
    tb = x_ref.shape[0]

    @pl.when(pl.program_id(0) == 0)
    def _():
        state_ref[...] = jnp.zeros_like(state_ref)

    lane = lax.broadcasted_iota(jnp.int32, (1, GROUP_W), 1)
    first = (lane % RET_HD) < (RET_HD // 2)
    cosf = jnp.concatenate([cos_ref[...], cos_ref[...]], axis=1)
    sinf = jnp.concatenate([sin_ref[...], sin_ref[...]], axis=1) * jnp.where(first, -1.0, 1.0)

    def rot(t):
        partner = jnp.where(first, pltpu.roll(t, GROUP_W - RET_HD // 2, 1), pltpu.roll(t, RET_HD // 2, 1))
        return t * cosf + partner * sinf

    q = rot(x_ref[:, 0:256]) * (RET_HD ** -0.5)
    k = rot(x_ref[:, 256:512])
    v = x_ref[:, 512:768]
    qdec = qdec_ref[...]
    kdec = kdec_ref[...]
    for c in range(tb // RET_CHUNK):
        r0 = c * RET_CHUNK
        qc, kc, vc = q[r0:r0 + RET_CHUNK], k[r0:r0 + RET_CHUNK], v[r0:r0 + RET_CHUNK]
        st = state_ref[...]
        o = _dot(qc, st) * qdec
        for h in range(RET_HEADS):
            hm = (lane // RET_HD) == h
            sc = _dot(jnp.where(hm, qc, 0.0), kc, _NT) * dmask_ref[h]
            o = o + _dot(sc, jnp.where(hm, vc, 0.0))
        kv = _dot(kc * kdec, vc, _TN)
        state_ref[...] = st * cdec_ref[...] + kv * bmask_ref[...]
        o_scr[r0:r0 + RET_CHUNK, :] = o
    o = o_scr[...]
    bones = bones_ref[...]
    mu = _dot(o, bones, hi=True) * (1.0 / RET_HD)
    d = o - mu
    var = _dot(d * d, bones, hi=True) * (1.0 / RET_HD)
    on = d * lax.rsqrt(var + EPS) * gn_ref[...]
    g = x_ref[:, 768:1024]
    y = g * jax.nn.sigmoid(g) * on
    out_ref[...] = _rms(y, bn_ref[...])


def _retention(ret_in, cos, sin, consts, gn, bn):
    s = ret_in.shape[0]
    tb = min(s, TB_RET)
    dmask, qdec, kdec, cdec, bmask, bones = consts
    return pl.pallas_call(
        _ret_kernel, grid=(s // tb,),
        in_specs=[_rows(tb, 1024), _rows(tb, 128), _rows(tb, 128), _full(dmask.shape), _full(qdec.shape),
                  _full(kdec.shape), _full(cdec.shape), _full(bmask.shape), _full(bones.shape),
                  _full((1, GROUP_W)), _full((1, GROUP_W))],
        out_specs=_rows(tb, GROUP_W),
        out_shape=jax.ShapeDtypeStruct((s, GROUP_W), f32),
        scratch_shapes=[pltpu.VMEM((GROUP_W, GROUP_W), f32), pltpu.VMEM((tb, GROUP_W), f32)],
        compiler_params=_params(), name="retention")(ret_in, cos, sin, dmask, qdec, kdec, cdec, bmask, bones, gn, bn)


def _lru_kernel(x_ref, cw_ref, cb_ref, wa_ref, ba_ref, wx_ref, bx_ref, lam_ref, bn_ref, out_ref, xbuf, carry_ref):
    tb = x_ref.shape[0]

    @pl.when(pl.program_id(0) == 0)
    def _():
        xbuf[0:8, :] = jnp.zeros((8, GROUP_W), f32)
        carry_ref[...] = jnp.zeros_like(carry_ref)

    x = x_ref[:, 256:512]
    xbuf[8:8 + tb, :] = x
    cw = cw_ref[...]
    xb = (cw[3:4] * x + cw[2:3] * xbuf[7:7 + tb, :] + cw[1:2] * xbuf[6:6 + tb, :]
          + cw[0:1] * xbuf[5:5 + tb, :]) + cb_ref[...]
    xbuf[0:8, :] = xbuf[tb:tb + 8, :]
    r = jax.nn.sigmoid(_dot(xb, wa_ref[...], hi=True) + ba_ref[...])
    i = jax.nn.sigmoid(_dot(xb, wx_ref[...], hi=True) + bx_ref[...])
    log_a = -LRU_C * r * jax.nn.softplus(-lam_ref[...])
    a = jnp.exp(log_a)
    y2 = 2.0 * log_a
    one_minus_a2 = -jnp.tanh(0.5 * y2) * (jnp.exp(y2) + 1.0)
    b = jnp.sqrt(one_minus_a2) * (i * xb)
    d = 1
    while d < tb:
        b = a * _shift_rows(b, d, 0.0) + b
        a = a * _shift_rows(a, d, 1.0)
        d *= 2
    h = b + a * carry_ref[...]
    carry_ref[...] = h[tb - 1:tb, :]
    y = jax.nn.gelu(x_ref[:, 0:256]) * h
    out_ref[...] = _rms(y, bn_ref[...])


def _rglru(lru_in, cw, cb, wa, ba, wx, bx, lam, bn):
    s = lru_in.shape[0]
    tb = min(s, TB_LRU)
    row = _full((1, GROUP_W))
    sq = _full((GROUP_W, GROUP_W))
    return pl.pallas_call(
        _lru_kernel, grid=(s // tb,),
        in_specs=[_rows(tb, 512), _full((CONV_K, GROUP_W)), row, sq, row, sq, row, row, row],
        out_specs=_rows(tb, GROUP_W),
        out_shape=jax.ShapeDtypeStruct((s, GROUP_W), f32),
        scratch_shapes=[pltpu.VMEM((tb + 8, GROUP_W), f32), pltpu.VMEM((1, GROUP_W), f32)],
        compiler_params=_params(), name="rglru")(lru_in, cw, cb, wa, ba, wx, bx, lam, bn)


def _s5_kernel(u_ref, a_ref, bre_ref, bim_ref, cre_ref, cim_ref, d_ref, gw_ref, gb_ref, bn_ref, out_ref, carry_ref):
    tb = u_ref.shape[0]

    @pl.when(pl.program_id(0) == 0)
    def _():
        carry_ref[...] = jnp.zeros_like(carry_ref)

    u = u_ref[...]
    ar = a_ref[0:1, :]
    ai = a_ref[1:2, :]
    xr = _dot(u, bre_ref[...])
    xi = _dot(u, bim_ref[...])
    cr = carry_ref[0:1, :]
    ci = carry_ref[1:2, :]
    row = lax.broadcasted_iota(jnp.int32, xr.shape, 0)
    xr = jnp.where(row == 0, xr + (ar * cr - ai * ci), xr)
    xi = jnp.where(row == 0, xi + (ar * ci + ai * cr), xi)
    d = 1
    while d < tb:
        sr = _shift_rows(xr, d, 0.0)
        si = _shift_rows(xi, d, 0.0)
        xr = xr + (ar * sr - ai * si)
        xi = xi + (ar * si + ai * sr)
        ar, ai = ar * ar - ai * ai, 2.0 * ar * ai
        d *= 2
    carry_ref[0:1, :] = xr[tb - 1:tb, :]
    carry_ref[1:2, :] = xi[tb - 1:tb, :]
    y = _dot(xr, cre_ref[...]) - _dot(xi, cim_ref[...]) + d_ref[...] * u
    zz = _dot(jax.nn.gelu(y), gw_ref[...]) + gb_ref[...]
    o = zz[:, 0:GROUP_W] * jax.nn.sigmoid(zz[:, GROUP_W:2 * GROUP_W])
    out_ref[...] = _rms(o, bn_ref[...])


def _s5(s5_in, a, bre, bim, cre, cim, dd, gw, gb, bn):
    s = s5_in.shape[0]
    tb = min(s, TB_S5)
    row = _full((1, GROUP_W))
    return pl.pallas_call(
        _s5_kernel, grid=(s // tb,),
        in_specs=[_rows(tb, GROUP_W), _full((2, S5_N)), _full((GROUP_W, S5_N)), _full((GROUP_W, S5_N)),
                  _full((S5_N, GROUP_W)), _full((S5_N, GROUP_W)), row, _full((GROUP_W, 2 * GROUP_W)),
                  _full((1, 2 * GROUP_W)), row],
        out_specs=_rows(tb, GROUP_W),
        out_shape=jax.ShapeDtypeStruct((s, GROUP_W), f32),
        scratch_shapes=[pltpu.VMEM((2, S5_N), f32)],
        compiler_params=_params(), name="s5")(s5_in, a, bre, bim, cre, cim, dd, gw, gb, bn)


def _gdn_kernel(x_ref, ba_ref, cw_ref, alog_ref, dtb_ref, eb_ref, ea_ref, bones_ref, nw_ref, bn_ref, out_ref,
                xbuf, state_ref, o_scr):
    tb = x_ref.shape[0]
    c_ = GDN_CHUNK

    @pl.when(pl.program_id(0) == 0)
    def _():
        xbuf[0:8, :] = jnp.zeros((8, 3 * GROUP_W), f32)
        state_ref[...] = jnp.zeros_like(state_ref)

    x = x_ref[:, 0:768]
    xbuf[8:8 + tb, :] = x
    cw = cw_ref[...]
    y = cw[3:4] * x + cw[2:3] * xbuf[7:7 + tb, :] + cw[1:2] * xbuf[6:6 + tb, :] + cw[0:1] * xbuf[5:5 + tb, :]
    xbuf[0:8, :] = xbuf[tb:tb + 8, :]
    y = y * jax.nn.sigmoid(y)
    bones = bones_ref[...]
    q = y[:, 0:256]
    k = y[:, 256:512]
    v = y[:, 512:768]
    q = q * lax.rsqrt(_dot(q * q, bones, hi=True) + EPS) * (GDN_HD ** -0.5)
    k = k * lax.rsqrt(_dot(k * k, bones, hi=True) + EPS)
    ba = ba_ref[...]
    beta_b = _dot(jax.nn.sigmoid(ba), eb_ref[...], hi=True)
    la = -jnp.exp(alog_ref[...]) * jax.nn.softplus(ba + dtb_ref[...])
    la_b = _dot(la, ea_ref[...], hi=True)

    ri = lax.broadcasted_iota(jnp.int32, (c_, c_), 0)
    ci = lax.broadcasted_iota(jnp.int32, (c_, c_), 1)
    incl = ri >= ci
    strict = ri > ci
    eye = ri == ci
    eye_f = eye.astype(f32)
    ltri = incl.astype(f32)
    ones = jnp.ones((c_, c_), f32)
    lev_masks = []
    b = 1
    while b < c_:
        lev_masks.append(((ri // (2 * b)) == (ci // (2 * b))) & (((ri // b) % 2) == 1) & (((ci // b) % 2) == 0))
        b *= 2

    for c in range(tb // c_):
        r0 = c * c_
        gcum = _dot(ltri, la_b[r0:r0 + c_, :], hi=True)
        for h in range(GDN_HEADS):
            cs = slice(h * GDN_HD, (h + 1) * GDN_HD)
            qh, kh, vh = q[r0:r0 + c_, cs], k[r0:r0 + c_, cs], v[r0:r0 + c_, cs]
            bh = beta_b[r0:r0 + c_, cs]
            gc = gcum[:, cs]
            gr = _dot(ones, jnp.where(eye, gc, 0.0), hi=True)
            lm = jnp.where(incl, jnp.exp(jnp.where(incl, gc - gr, 0.0)), 0.0)
            kb = kh * bh
            lmat = jnp.where(strict, _dot(kb, kh, _NT) * lm, 0.0)
            t = eye_f
            for m in lev_masks:
                t = t - _dot(t, _dot(jnp.where(m, lmat, 0.0), t, hi=True), hi=True)
            eg = jnp.exp(gc)
            u = _dot(t, vh * bh, hi=True)
            w = _dot(t, kb * eg, hi=True)
            attn = jnp.where(incl, _dot(qh, kh, _NT) * lm, 0.0)
            gl = gc[c_ - 1:c_, :]
            kd = kh * jnp.exp(gl - gc)
            cd = jnp.exp(gl)
            mm = jnp.where(eye, cd, 0.0) - _dot(kd, w, _TN, hi=True)
            nn = _dot(kd, u, _TN, hi=True)
            pp = _dot(attn, u)
            rr = qh * eg - _dot(attn, w)
            st = state_ref[h]
            o_scr[r0:r0 + c_, cs] = pp + _dot(rr, st, hi=True)
            state_ref[h] = _dot(mm, st, hi=True) + nn
    o = o_scr[...]
    on = o * lax.rsqrt(_dot(o * o, bones, hi=True) * (1.0 / GDN_HD) + EPS) * nw_ref[...]
    g = x_ref[:, 768:1024]
    yy = on * (g * jax.nn.sigmoid(g))
    out_ref[...] = _rms(yy, bn_ref[...])


def _gdn(gdn_in, ba, cw, alog, dtb, eb, ea, bones, nw, bn):
    s = gdn_in.shape[0]
    tb = min(s, TB_GDN)
    row = _full((1, GROUP_W))
    return pl.pallas_call(
        _gdn_kernel, grid=(s // tb,),
        in_specs=[_rows(tb, 1024), _rows(tb, 128), _full((CONV_K, 3 * GROUP_W)), _full((1, 128)), _full((1, 128)),
                  _full((128, GROUP_W)), _full((128, GROUP_W)), _full((GROUP_W, GROUP_W)), row, row],
        out_specs=_rows(tb, GROUP_W),
        out_shape=jax.ShapeDtypeStruct((s, GROUP_W), f32),
        scratch_shapes=[pltpu.VMEM((tb + 8, 3 * GROUP_W), f32), pltpu.VMEM((GDN_HEADS, GDN_HD, GDN_HD), f32),
                        pltpu.VMEM((tb, GROUP_W), f32)],
        compiler_params=_params(), name="gated_deltanet")(gdn_in, ba, cw, alog, dtb, eb, ea, bones, nw, bn)


def _out_kernel(h_ref, m0_ref, m1_ref, m2_ref, m3_ref, w_ref, out_ref):
    acc = h_ref[...]
    for j, m in enumerate((m0_ref, m1_ref, m2_ref, m3_ref)):
        acc = acc + jnp.dot(m[...].astype(bf16), w_ref[j * GROUP_W:(j + 1) * GROUP_W, :], preferred_element_type=f32)
    out_ref[...] = acc


def _out_proj(h, mixed, w):
    s = h.shape[0]
    tb = min(s, TB_OUT)
    return pl.pallas_call(
        _out_kernel, grid=(s // tb,),
        in_specs=[_rows(tb, D_MODEL)] + [_rows(tb, GROUP_W)] * 4 + [_full((D_MODEL, D_MODEL))],
        out_specs=_rows(tb, D_MODEL),
        out_shape=jax.ShapeDtypeStruct((s, D_MODEL), f32),
        compiler_params=_params(), name="out_proj")(h, *mixed, w)


def _extract_topk(s, idx_col, n_bad, put):
    for r in range(PEER_TOPK):
        m = jnp.max(s, axis=0, keepdims=True)
        idx = jnp.min(jnp.where(s == m, idx_col, n_bad), axis=0, keepdims=True)
        oh = idx_col == idx
        put(r, m, oh)
        s = jnp.where(oh, -jnp.inf, s)


def _topk_kernel(h_ref, nw_ref, wqt_ref, sk_ref, p1_ref, p2_ref, cidx_ref, qsel_ref,
                 zt_ref, r2_ref, jc_ref, e1_ref, e2_ref, s_scr, rank_scr, val_scr):
    tb = h_ref.shape[0]
    z = _rms(h_ref[...], nw_ref[...])
    zt = z.T.astype(bf16)
    zt_ref[...] = zt
    qt = jnp.dot(wqt_ref[...], zt, preferred_element_type=f32)
    for hp in range(2 * PEER_HEADS):
        s_scr[hp] = _dot(sk_ref[hp], qt[hp * PEER_HALF:(hp + 1) * PEER_HALF, :])

    key_idx = lax.broadcasted_iota(jnp.int32, (PEER_NKEYS, tb), 0).astype(f32)

    def stage1(hp, carry):
        rank = [jnp.full((PEER_NKEYS, tb), float(PEER_TOPK), f32)]

        def put(r, m, oh):
            val_scr[hp, r:r + 1, :] = m
            rank[0] = jnp.where(oh, float(r), rank[0])

        _extract_topk(s_scr[hp], key_idx, float(PEER_NKEYS), put)
        rank_scr[hp] = rank[0]
        return carry

    lax.fori_loop(0, 2 * PEER_HEADS, stage1, 0)

    cidx = jnp.broadcast_to(cidx_ref[...], (PEER_NCAND, tb))
    p1 = p1_ref[...]
    p2 = p2_ref[...]

    def stage2(h, carry):
        v1 = val_scr[2 * h]
        v2 = val_scr[2 * h + 1]
        cand = _dot(p1, v1, hi=True) + _dot(p2, v2, hi=True)
        cand = jnp.where(cidx < 256.0, cand, -jnp.inf)
        sel = [jnp.zeros((PEER_NCAND, tb), f32)]

        def put(r, m, oh):
            sel[0] = jnp.where(oh, 1.0, sel[0])

        _extract_topk(cand, cidx, 1e9, put)
        m1 = v1[0:1, :]
        m2 = v2[0:1, :]
        wgt = sel[0] * _dot(p1, jnp.exp(v1 - m1), hi=True) * _dot(p2, jnp.exp(v2 - m2), hi=True)
        zsum = jnp.sum(wgt, axis=0, keepdims=True)
        cnt = _dot(qsel_ref[...], sel[0], hi=True)
        r1 = rank_scr[2 * h]
        r2 = rank_scr[2 * h + 1]
        jc = jnp.zeros((PEER_NKEYS, tb), f32)
        for i in range(PEER_TOPK):
            jc = jnp.where(r1 == float(i), cnt[i:i + 1, :], jc)
        r2_ref[h] = r2
        jc_ref[h] = jc
        e1_ref[h] = jnp.where(r1 < float(PEER_TOPK), jnp.exp(s_scr[2 * h] - m1), 0.0)
        e2_ref[h] = jnp.where(r2 < float(PEER_TOPK), jnp.exp(s_scr[2 * h + 1] - m2), 0.0) / zsum
        return carry

    lax.fori_loop(0, PEER_HEADS, stage2, 0)


def _peer_topk(h, nw, wqt, sk, p1, p2, cidx, qsel):
    s = h.shape[0]
    tb = min(s, TB_TOPK)
    gate_spec = pl.BlockSpec((PEER_HEADS, PEER_NKEYS, tb), lambda i: (0, 0, i))
    gate_shape = jax.ShapeDtypeStruct((PEER_HEADS, PEER_NKEYS, s), f32)
    return pl.pallas_call(
        _topk_kernel, grid=(s // tb,),
        in_specs=[_rows(tb, D_MODEL), _full((1, D_MODEL)), _full((D_MODEL, D_MODEL)),
                  _full((2 * PEER_HEADS, PEER_NKEYS, PEER_HALF)), _full((PEER_NCAND, PEER_TOPK)),
                  _full((PEER_NCAND, PEER_TOPK)), _full((PEER_NCAND, 1)), _full((PEER_TOPK, PEER_NCAND))],
        out_specs=[pl.BlockSpec((D_MODEL, tb), lambda i: (0, i)), gate_spec, gate_spec, gate_spec, gate_spec],
        out_shape=[jax.ShapeDtypeStruct((D_MODEL, s), bf16), gate_shape, gate_shape, gate_shape, gate_shape],
        scratch_shapes=[pltpu.VMEM((2 * PEER_HEADS, PEER_NKEYS, tb), f32),
                        pltpu.VMEM((2 * PEER_HEADS, PEER_NKEYS, tb), f32),
                        pltpu.VMEM((2 * PEER_HEADS, PEER_TOPK, tb), f32)],
        compiler_params=_params(), name="peer_topk")(h, nw, wqt, sk, p1, p2, cidx, qsel)


def _peer_kernel(zt_ref, h_ref, u_ref, vt_ref, r2_ref, jc_ref, e1_ref, e2_ref, out_ref, acc_ref, hh_ref):
    j = pl.program_id(1)
    tb = h_ref.shape[0]
    ne = u_ref.shape[0]
    na = ne // PEER_NKEYS

    @pl.when(j == 0)
    def _():
        acc_ref[...] = jnp.zeros_like(acc_ref)

    act = jnp.dot(u_ref[...], zt_ref[...], preferred_element_type=f32)
    for ai in range(na):
        a = j * na + ai
        g = jnp.zeros((PEER_NKEYS, tb), f32)
        for h in range(PEER_HEADS):
            jc = jc_ref[h, pl.ds(a, 1), :]
            e1 = e1_ref[h, pl.ds(a, 1), :]
            g = g + jnp.where(r2_ref[h] < jc, e2_ref[h] * e1, 0.0)
        r0 = ai * PEER_NKEYS
        hh_ref[r0:r0 + PEER_NKEYS, :] = (g * jax.nn.gelu(act[r0:r0 + PEER_NKEYS, :])).astype(bf16)
    acc_ref[...] += jnp.dot(vt_ref[...], hh_ref[...], preferred_element_type=f32)

    @pl.when(j == pl.num_programs(1) - 1)
    def _():
        out_ref[...] = h_ref[...] + acc_ref[...].T


def _peer_dense(zt, h, u, vt, r2, jc, e1, e2):
    s = h.shape[0]
    tb = min(s, TB_PEER)
    n_exp = u.shape[0]
    gate_spec = pl.BlockSpec((PEER_HEADS, PEER_NKEYS, tb), lambda i, j: (0, 0, i))
    return pl.pallas_call(
        _peer_kernel, grid=(s // tb, n_exp // NE_TILE),
        in_specs=[pl.BlockSpec((D_MODEL, tb), lambda i, j: (0, i)), pl.BlockSpec((tb, D_MODEL), lambda i, j: (i, 0)),
                  pl.BlockSpec((NE_TILE, D_MODEL), lambda i, j: (j, 0)),
                  pl.BlockSpec((D_MODEL, NE_TILE), lambda i, j: (0, j)),
                  gate_spec, gate_spec, gate_spec, gate_spec],
        out_specs=pl.BlockSpec((tb, D_MODEL), lambda i, j: (i, 0)),
        out_shape=jax.ShapeDtypeStruct((s, D_MODEL), f32),
        scratch_shapes=[pltpu.VMEM((D_MODEL, tb), f32), pltpu.VMEM((NE_TILE, tb), bf16)],
        compiler_params=_params(2), name="peer_dense")(zt, h, u, vt, r2, jc, e1, e2)


def _ple_kernel(h_ref, p_ref, nw_ref, wg_ref, wp_ref, fw_ref, out_ref, *, final):
    x = h_ref[...]
    gate = jax.nn.sigmoid(jnp.dot(_rms(x, nw_ref[...]).astype(bf16), wg_ref[...], preferred_element_type=f32))
    y = x + jnp.dot(p_ref[...].astype(bf16), wp_ref[...], preferred_element_type=f32) * gate
    if final:
        y = _rms(y, fw_ref[...])
    out_ref[...] = y


def _ple(h, p, nw, wg, wp, fw, final):
    s = h.shape[0]
    tb = min(s, TB_PLE)
    pw = p.shape[1]
    return pl.pallas_call(
        functools.partial(_ple_kernel, final=final), grid=(s // tb,),
        in_specs=[_rows(tb, D_MODEL), _rows(tb, pw), _full((1, D_MODEL)), _full((D_MODEL, D_MODEL)),
                  _full((pw, D_MODEL)), _full((1, D_MODEL))],
        out_specs=_rows(tb, D_MODEL),
        out_shape=jax.ShapeDtypeStruct((s, D_MODEL), f32),
        compiler_params=_params(), name="ple_gate")(h, p, nw, wg, wp, fw)


def _block_diag(blocks):
    n, r, c = blocks.shape
    return jnp.einsum('nrc,nm->nrmc', blocks, jnp.eye(n, dtype=blocks.dtype)).reshape(n * r, n * c)


def _retention_consts():
    log_gamma = jnp.log(1.0 - 2.0 ** (-5.0 - jnp.arange(RET_HEADS, dtype=f32)))
    idx = jnp.arange(RET_CHUNK, dtype=f32)
    diff = idx[:, None] - idx[None, :]
    causal = diff >= 0
    dmask = jnp.where(causal, jnp.exp(log_gamma[:, None, None] * jnp.where(causal, diff, 0.0)), 0.0)
    q_decay = jnp.exp(log_gamma[:, None] * (idx + 1.0))
    k_decay = jnp.exp(log_gamma[:, None] * (RET_CHUNK - 1.0 - idx))
    chunk_decay = jnp.exp(log_gamma * RET_CHUNK)
    qdec = jnp.repeat(q_decay.T, RET_HD, axis=1)
    kdec = jnp.repeat(k_decay.T, RET_HD, axis=1)
    bmask = _block_diag(jnp.ones((RET_HEADS, RET_HD, RET_HD), f32))
    cdec = _block_diag(jnp.broadcast_to(chunk_decay[:, None, None], (RET_HEADS, RET_HD, RET_HD)))
    return dmask, qdec, kdec, cdec, bmask, bmask


def _s5_discretize(a_re, a_im, b_re, b_im, c_re, c_im, log_dt):
    dt = jnp.exp(log_dt)[:, None]
    mag = jnp.exp(a_re * dt)
    ang = a_im * dt
    ab_re, ab_im = mag * jnp.cos(ang), mag * jnp.sin(ang)
    den = a_re * a_re + a_im * a_im
    p_re, p_im = ab_re - 1.0, ab_im
    f_re = (p_re * a_re + p_im * a_im) / den
    f_im = (p_im * a_re - p_re * a_im) / den
    bb_re = f_re[..., None] * b_re - f_im[..., None] * b_im
    bb_im = f_re[..., None] * b_im + f_im[..., None] * b_re
    abar = jnp.stack([ab_re.reshape(-1), ab_im.reshape(-1)])
    bre = _block_diag(jnp.swapaxes(bb_re, 1, 2))
    bim = _block_diag(jnp.swapaxes(bb_im, 1, 2))
    cre = _block_diag(jnp.swapaxes(c_re, 1, 2))
    cim = _block_diag(jnp.swapaxes(c_im, 1, 2))
    return abar, bre, bim, cre, cim


def _peer_rank_consts():
    pairs = [(i, j) for i in range(PEER_TOPK) for j in range(PEER_TOPK) if (i + 1) * (j + 1) <= PEER_TOPK]
    p1 = np.zeros((PEER_NCAND, PEER_TOPK), np.float32)
    p2 = np.zeros((PEER_NCAND, PEER_TOPK), np.float32)
    cidx = np.full((PEER_NCAND, 1), 1e9, np.float32)
    for c, (i, j) in enumerate(pairs):
        p1[c, i] = 1.0
        p2[c, j] = 1.0
        cidx[c, 0] = PEER_TOPK * i + j
    return jnp.asarray(p1), jnp.asarray(p2), jnp.asarray(cidx), jnp.asarray(p1.T)


def _row(v):
    return v.reshape(1, -1).astype(f32)


def kernel(x, p, positions, mix_norm, w_in, ret_gn, lru_conv_w, lru_conv_b, lru_w_a, lru_b_a, lru_w_x, lru_b_x, lru_lambda, s5_a_re, s5_a_im, s5_b_re, s5_b_im, s5_c_re, s5_c_im, s5_d, s5_log_dt, s5_glu_w, s5_glu_b, gdn_conv_w, gdn_a_log, gdn_dt_bias, gdn_norm, branch_norm, w_out, ffn_norm, peer_wq, peer_subkeys, peer_u, peer_v, ple_norm, ple_wg, ple_wp, final_norm):
    bsz, seq, _ = x.shape
    depth = w_in.shape[0]
    assert bsz == 1
    h = x.reshape(seq, D_MODEL)

    half = RET_HD // 2
    inv_freq = ROPE_BASE ** (-jnp.arange(half, dtype=f32) / half)
    freq = jnp.tile(inv_freq, 128 // half).reshape(1, 128)
    cos, sin = _rope_tables(positions.reshape(seq, 1).astype(f32), freq)

    ret_consts = _retention_consts()
    bones = ret_consts[5]
    p1, p2, cidx, qsel = _peer_rank_consts()
    head_rows = jnp.arange(128)[:, None]
    head_lanes = jnp.arange(GROUP_W)[None, :] // GDN_HD
    eb = (head_rows == head_lanes).astype(f32)
    ea = (head_rows == head_lanes + GDN_HEADS).astype(f32)

    for l in range(depth):
        w = w_in[l]
        w_perm = jnp.concatenate([w[:, 0:2560], w[:, 2568:2824], w[:, 2560:2568],
                                  jnp.zeros((D_MODEL, IN_PAD - 2824), f32)], axis=1).astype(bf16)
        ret_in, lru_in, s5_in, gdn_in, ba_in = _in_proj(h, _row(mix_norm[l]), w_perm)

        y_ret = _retention(ret_in, cos, sin, ret_consts, _row(ret_gn[l]), _row(branch_norm[l, 0]))
        y_lru = _rglru(lru_in, lru_conv_w[l], _row(lru_conv_b[l]), _block_diag(lru_w_a[l]), _row(lru_b_a[l]),
                       _block_diag(lru_w_x[l]), _row(lru_b_x[l]), _row(lru_lambda[l]), _row(branch_norm[l, 1]))
        abar, bre, bim, cre, cim = _s5_discretize(s5_a_re[l], s5_a_im[l], s5_b_re[l], s5_b_im[l],
                                                  s5_c_re[l], s5_c_im[l], s5_log_dt[l])
        y_s5 = _s5(s5_in, abar, bre, bim, cre, cim, _row(s5_d[l]), s5_glu_w[l], _row(s5_glu_b[l]),
                   _row(branch_norm[l, 2]))
        alog = jnp.zeros((1, 128), f32).at[0, GDN_HEADS:2 * GDN_HEADS].set(gdn_a_log[l])
        dtb = jnp.zeros((1, 128), f32).at[0, GDN_HEADS:2 * GDN_HEADS].set(gdn_dt_bias[l])
        y_gdn = _gdn(gdn_in, ba_in, gdn_conv_w[l], alog, dtb, eb, ea, bones,
                     _row(jnp.tile(gdn_norm[l], GDN_HEADS)), _row(branch_norm[l, 3]))

        h = _out_proj(h, (y_ret, y_lru, y_s5, y_gdn), w_out[l].astype(bf16))

        sk = peer_subkeys[l].reshape(2 * PEER_HEADS, PEER_NKEYS, PEER_HALF)
        zt, r2, jc, e1, e2 = _peer_topk(h, _row(ffn_norm[l]), peer_wq[l].T.astype(bf16), sk, p1, p2, cidx, qsel)
        h = _peer_dense(zt, h, peer_u[l].astype(bf16), peer_v[l].T.astype(bf16), r2, jc, e1, e2)

        h = _ple(h, p[l].reshape(seq, -1), _row(ple_norm[l]), ple_wg[l].astype(bf16), ple_wp[l].astype(bf16),
                 _row(final_norm), final=(l == depth - 1))
    return h.reshape(bsz, seq, D_MODEL)
```

```python
import functools
import math

import jax
import jax.numpy as jnp
import numpy as np
from jax import lax
from jax.experimental import pallas as pl
from jax.experimental.pallas import tpu as pltpu

f32 = jnp.float32
bf16 = jnp.bfloat16
_HI = lax.Precision.HIGHEST
_NT = (((1,), (1,)), ((), ()))
_TN = (((0,), (0,)), ((), ()))

EPS = 1e-6
D_MODEL = 1024
GROUP_W = 256
CONV_K = 4
RET_HEADS = 4
RET_HD = 64
RET_CHUNK = 128
ROPE_BASE = 10000.0
LRU_C = 8.0
S5_GROUPS = 16
S5_GW = 16
S5_STATE = 64
S5_N = S5_GROUPS * S5_STATE
GDN_HEADS = 4
GDN_HD = 64
GDN_CHUNK = 64
PEER_HEADS = 8
PEER_NKEYS = 128
PEER_HALF = 64
PEER_TOPK = 16
PEER_NCAND = 64

IN_PAD = 2944
VMEM_LIMIT = 48 * 1024 * 1024

TB_IN = 256
TB_RET = 512
TB_LRU = 512
TB_S5 = 512
TB_GDN = 256
TB_OUT = 512
TB_TOPK = 256
TB_PEER = 512
NE_TILE = 512
TB_PLE = 512


def _dot(a, b, dims=(((1,), (0,)), ((), ())), hi=False):
    if hi:
        return lax.dot_general(a, b, dims, precision=_HI, preferred_element_type=f32)
    return lax.dot_general(a.astype(bf16), b.astype(bf16), dims, preferred_element_type=f32)


def _rms(x, w):
    return x * lax.rsqrt(jnp.mean(x * x, axis=-1, keepdims=True) + EPS) * w


def _shift_rows(x, d, fill):
    row = lax.broadcasted_iota(jnp.int32, x.shape, 0)
    return jnp.where(row >= d, pltpu.roll(x, d, 0), fill)


def _params(n_axes=1):
    return pltpu.CompilerParams(dimension_semantics=("arbitrary",) * n_axes, vmem_limit_bytes=VMEM_LIMIT)


def _full(shape):
    n = len(shape)
    return pl.BlockSpec(shape, lambda *_: (0,) * n)


def _rows(tb, w):
    return pl.BlockSpec((tb, w), lambda i: (i, 0))


def _rope_kernel(pos_ref, freq_ref, cos_ref, sin_ref):
    ang = pos_ref[...] * freq_ref[...]
    cos_ref[...] = jnp.cos(ang)
    sin_ref[...] = jnp.sin(ang)


def _rope_tables(pos, freq):
    s = pos.shape[0]
    tb = min(s, 1024)
    return pl.pallas_call(
        _rope_kernel, grid=(s // tb,),
        in_specs=[_rows(tb, 1), _full((1, 128))],
        out_specs=[_rows(tb, 128), _rows(tb, 128)],
        out_shape=[jax.ShapeDtypeStruct((s, 128), f32)] * 2,
        compiler_params=_params(), name="rope_tables")(pos, freq)


def _in_kernel(h_ref, nw_ref, w_ref, ret_ref, lru_ref, s5_ref, gdn_ref, ba_ref):
    z = _rms(h_ref[...], nw_ref[...]).astype(bf16)
    off = 0
    for ref in (ret_ref, lru_ref, s5_ref, gdn_ref, ba_ref):
        w = ref.shape[1]
        ref[...] = jnp.dot(z, w_ref[:, off:off + w], preferred_element_type=f32)
        off += w


def _in_proj(h, nw, w):
    s = h.shape[0]
    tb = min(s, TB_IN)
    widths = (1024, 512, 256, 1024, 128)
    return pl.pallas_call(
        _in_kernel, grid=(s // tb,),
        in_specs=[_rows(tb, D_MODEL), _full((1, D_MODEL)), _full((D_MODEL, IN_PAD))],
        out_specs=[_rows(tb, w_) for w_ in widths],
        out_shape=[jax.ShapeDtypeStruct((s, w_), f32) for w_ in widths],
        compiler_params=_params(), name="in_proj")(h, nw, w)


def _ret_kernel(x_ref, cos_ref, sin_ref, dmask_ref, qdec_ref, kdec_ref, cdec_ref, bmask_ref, bones_ref,
                gn_ref, bn_ref, out_ref, state_ref, o_scr):
    tb = x_ref.shape[0]

    @pl.when(pl.program_id(0) == 0)
    def _():
        state_ref[...] = jnp.zeros_like(state_ref)

    lane = lax.broadcasted_iota(jnp.int32, (1, GROUP_W), 1)
    first = (lane % RET_HD) < (RET_HD // 2)
    cosf = jnp.concatenate([cos_ref[...], cos_ref[...]], axis=1)
    sinf = jnp.concatenate([sin_ref[...], sin_ref[...]], axis=1) * jnp.where(first, -1.0, 1.0)

    def rot(t):
        partner = jnp.where(first, pltpu.roll(t, GROUP_W - RET_HD // 2, 1), pltpu.roll(t, RET_HD // 2, 1))
        return t * cosf + partner * sinf

    q = rot(x_ref[:, 0:256]) * (RET_HD ** -0.5)
    k = rot(x_ref[:, 256:512])
    v = x_ref[:, 512:768]
    qdec = qdec_ref[...]
    kdec = kdec_ref[...]
    for c in range(tb // RET_CHUNK):
        r0 = c * RET_CHUNK
        qc, kc, vc = q[r0:r0 + RET_CHUNK], k[r0:r0 + RET_CHUNK], v[r0:r0 + RET_CHUNK]
        st = state_ref[...]
        o = _dot(qc, st) * qdec
        for h in range(RET_HEADS):
            hm = (lane // RET_HD) == h
            sc = _dot(jnp.where(hm, qc, 0.0), kc, _NT) * dmask_ref[h]
            o = o + _dot(sc, jnp.where(hm, vc, 0.0))
        kv = _dot(kc * kdec, vc, _TN)
        state_ref[...] = st * cdec_ref[...] + kv * bmask_ref[...]
        o_scr[r0:r0 + RET_CHUNK, :] = o
    o = o_scr[...]
    bones = bones_ref[...]
    mu = _dot(o, bones, hi=True) * (1.0 / RET_HD)
    d = o - mu
    var = _dot(d * d, bones, hi=True) * (1.0 / RET_HD)
    on = d * lax.rsqrt(var + EPS) * gn_ref[...]
    g = x_ref[:, 768:1024]
    y = g * jax.nn.sigmoid(g) * on
    out_ref[...] = _rms(y, bn_ref[...])


def _retention(ret_in, cos, sin, consts, gn, bn):
    s = ret_in.shape[0]
    tb = min(s, TB_RET)
    dmask, qdec, kdec, cdec, bmask, bones = consts
    return pl.pallas_call(
        _ret_kernel, grid=(s // tb,),
        in_specs=[_rows(tb, 1024), _rows(tb, 128), _rows(tb, 128), _full(dmask.shape), _full(qdec.shape),
                  _full(kdec.shape), _full(cdec.shape), _full(bmask.shape), _full(bones.shape),
                  _full((1, GROUP_W)), _full((1, GROUP_W))],
        out_specs=_rows(tb, GROUP_W),
        out_shape=jax.ShapeDtypeStruct((s, GROUP_W), f32),
        scratch_shapes=[pltpu.VMEM((GROUP_W, GROUP_W), f32), pltpu.VMEM((tb, GROUP_W), f32)],
        compiler_params=_params(), name="retention")(ret_in, cos, sin, dmask, qdec, kdec, cdec, bmask, bones, gn, bn)


def _lru_kernel(x_ref, cw_ref, cb_ref, wa_ref, ba_ref, wx_ref, bx_ref, lam_ref, bn_ref, out_ref, xbuf, carry_ref):
    tb = x_ref.shape[0]

    @pl.when(pl.program_id(0) == 0)
    def _():
        xbuf[0:8, :] = jnp.zeros((8, GROUP_W), f32)
        carry_ref[...] = jnp.zeros_like(carry_ref)

    x = x_ref[:, 256:512]
    xbuf[8:8 + tb, :] = x
    cw = cw_ref[...]
    xb = (cw[3:4] * x + cw[2:3] * xbuf[7:7 + tb, :] + cw[1:2] * xbuf[6:6 + tb, :]
          + cw[0:1] * xbuf[5:5 + tb, :]) + cb_ref[...]
    xbuf[0:8, :] = xbuf[tb:tb + 8, :]
    r = jax.nn.sigmoid(_dot(xb, wa_ref[...], hi=True) + ba_ref[...])
    i = jax.nn.sigmoid(_dot(xb, wx_ref[...], hi=True) + bx_ref[...])
    log_a = -LRU_C * r * jax.nn.softplus(-lam_ref[...])
    a = jnp.exp(log_a)
    y2 = 2.0 * log_a
    one_minus_a2 = -jnp.tanh(0.5 * y2) * (jnp.exp(y2) + 1.0)
    b = jnp.sqrt(one_minus_a2) * (i * xb)
    d = 1
    while d < tb:
        b = a * _shift_rows(b, d, 0.0) + b
        a = a * _shift_rows(a, d, 1.0)
        d *= 2
    h = b + a * carry_ref[...]
    carry_ref[...] = h[tb - 1:tb, :]
    y = jax.nn.gelu(x_ref[:, 0:256]) * h
    out_ref[...] = _rms(y, bn_ref[...])


def _rglru(lru_in, cw, cb, wa, ba, wx, bx, lam, bn):
    s = lru_in.shape[0]
    tb = min(s, TB_LRU)
    row = _full((1, GROUP_W))
    sq = _full((GROUP_W, GROUP_W))
    return pl.pallas_call(
        _lru_kernel, grid=(s // tb,),
        in_specs=[_rows(tb, 512), _full((CONV_K, GROUP_W)), row, sq, row, sq, row, row, row],
        out_specs=_rows(tb, GROUP_W),
        out_shape=jax.ShapeDtypeStruct((s, GROUP_W), f32),
        scratch_shapes=[pltpu.VMEM((tb + 8, GROUP_W), f32), pltpu.VMEM((1, GROUP_W), f32)],
        compiler_params=_params(), name="rglru")(lru_in, cw, cb, wa, ba, wx, bx, lam, bn)


def _s5_kernel(u_ref, a_ref, bre_ref, bim_ref, cre_ref, cim_ref, d_ref, gw_ref, gb_ref, bn_ref, out_ref, carry_ref):
    tb = u_ref.shape[0]

    @pl.when(pl.program_id(0) == 0)
    def _():
        carry_ref[...] = jnp.zeros_like(carry_ref)

    u = u_ref[...]
    ar = a_ref[0:1, :]
    ai = a_ref[1:2, :]
    xr = _dot(u, bre_ref[...])
    xi = _dot(u, bim_ref[...])
    cr = carry_ref[0:1, :]
    ci = carry_ref[1:2, :]
    row = lax.broadcasted_iota(jnp.int32, xr.shape, 0)
    xr = jnp.where(row == 0, xr + (ar * cr - ai * ci), xr)
    xi = jnp.where(row == 0, xi + (ar * ci + ai * cr), xi)
    d = 1
    while d < tb:
        sr = _shift_rows(xr, d, 0.0)
        si = _shift_rows(xi, d, 0.0)
        xr = xr + (ar * sr - ai * si)
        xi = xi + (ar * si + ai * sr)
        ar, ai = ar * ar - ai * ai, 2.0 * ar * ai
        d *= 2
    carry_ref[0:1, :] = xr[tb - 1:tb, :]
    carry_ref[1:2, :] = xi[tb - 1:tb, :]
    y = _dot(xr, cre_ref[...]) - _dot(xi, cim_ref[...]) + d_ref[...] * u
    zz = _dot(jax.nn.gelu(y), gw_ref[...]) + gb_ref[...]
    o = zz[:, 0:GROUP_W] * jax.nn.sigmoid(zz[:, GROUP_W:2 * GROUP_W])
    out_ref[...] = _rms(o, bn_ref[...])


def _s5(s5_in, a, bre, bim, cre, cim, dd, gw, gb, bn):
    s = s5_in.shape[0]
    tb = min(s, TB_S5)
    row = _full((1, GROUP_W))
    return pl.pallas_call(
        _s5_kernel, grid=(s // tb,),
        in_specs=[_rows(tb, GROUP_W), _full((2, S5_N)), _full((GROUP_W, S5_N)), _full((GROUP_W, S5_N)),
                  _full((S5_N, GROUP_W)), _full((S5_N, GROUP_W)), row, _full((GROUP_W, 2 * GROUP_W)),
                  _full((1, 2 * GROUP_W)), row],
        out_specs=_rows(tb, GROUP_W),
        out_shape=jax.ShapeDtypeStruct((s, GROUP_W), f32),
        scratch_shapes=[pltpu.VMEM((2, S5_N), f32)],
        compiler_params=_params(), name="s5")(s5_in, a, bre, bim, cre, cim, dd, gw, gb, bn)


def _gdn_kernel(x_ref, ba_ref, cw_ref, alog_ref, dtb_ref, eb_ref, ea_ref, bones_ref, nw_ref, bn_ref, out_ref,
                xbuf, state_ref, o_scr):
    tb = x_ref.shape[0]
    c_ = GDN_CHUNK

    @pl.when(pl.program_id(0) == 0)
    def _():
        xbuf[0:8, :] = jnp.zeros((8, 3 * GROUP_W), f32)
        state_ref[...] = jnp.zeros_like(state_ref)

    x = x_ref[:, 0:768]
    xbuf[8:8 + tb, :] = x
    cw = cw_ref[...]
    y = cw[3:4] * x + cw[2:3] * xbuf[7:7 + tb, :] + cw[1:2] * xbuf[6:6 + tb, :] + cw[0:1] * xbuf[5:5 + tb, :]
    xbuf[0:8, :] = xbuf[tb:tb + 8, :]
    y = y * jax.nn.sigmoid(y)
    bones = bones_ref[...]
    q = y[:, 0:256]
    k = y[:, 256:512]
    v = y[:, 512:768]
    q = q * lax.rsqrt(_dot(q * q, bones, hi=True) + EPS) * (GDN_HD ** -0.5)
    k = k * lax.rsqrt(_dot(k * k, bones, hi=True) + EPS)
    ba = ba_ref[...]
    beta_b = _dot(jax.nn.sigmoid(ba), eb_ref[...], hi=True)
    la = -jnp.exp(alog_ref[...]) * jax.nn.softplus(ba + dtb_ref[...])
    la_b = _dot(la, ea_ref[...], hi=True)

    ri = lax.broadcasted_iota(jnp.int32, (c_, c_), 0)
    ci = lax.broadcasted_iota(jnp.int32, (c_, c_), 1)
    incl = ri >= ci
    strict = ri > ci
    eye = ri == ci
    eye_f = eye.astype(f32)
    ltri = incl.astype(f32)
    ones = jnp.ones((c_, c_), f32)
    lev_masks = []
    b = 1
    while b < c_:
        lev_masks.append(((ri // (2 * b)) == (ci // (2 * b))) & (((ri // b) % 2) == 1) & (((ci // b) % 2) == 0))
        b *= 2

    for c in range(tb // c_):
        r0 = c * c_
        gcum = _dot(ltri, la_b[r0:r0 + c_, :], hi=True)
        for h in range(GDN_HEADS):
            cs = slice(h * GDN_HD, (h + 1) * GDN_HD)
            qh, kh, vh = q[r0:r0 + c_, cs], k[r0:r0 + c_, cs], v[r0:r0 + c_, cs]
            bh = beta_b[r0:r0 + c_, cs]
            gc = gcum[:, cs]
            gr = _dot(ones, jnp.where(eye, gc, 0.0), hi=True)
            lm = jnp.where(incl, jnp.exp(jnp.where(incl, gc - gr, 0.0)), 0.0)
            kb = kh * bh
            lmat = jnp.where(strict, _dot(kb, kh, _NT) * lm, 0.0)
            t = eye_f
            for m in lev_masks:
                t = t - _dot(t, _dot(jnp.where(m, lmat, 0.0), t, hi=True), hi=True)
            eg = jnp.exp(gc)
            u = _dot(t, vh * bh, hi=True)
            w = _dot(t, kb * eg, hi=True)
            attn = jnp.where(incl, _dot(qh, kh, _NT) * lm, 0.0)
            gl = gc[c_ - 1:c_, :]
            kd = kh * jnp.exp(gl - gc)
            cd = jnp.exp(gl)
            mm = jnp.where(eye, cd, 0.0) - _dot(kd, w, _TN, hi=True)
            nn = _dot(kd, u, _TN, hi=True)
            pp = _dot(attn, u)
            rr = qh * eg - _dot(attn, w)
            st = state_ref[h]
            o_scr[r0:r0 + c_, cs] = pp + _dot(rr, st, hi=True)
            state_ref[h] = _dot(mm, st, hi=True) + nn
    o = o_scr[...]
    on = o * lax.rsqrt(_dot(o * o, bones, hi=True) * (1.0 / GDN_HD) + EPS) * nw_ref[...]
    g = x_ref[:, 768:1024]
    yy = on * (g * jax.nn.sigmoid(g))
    out_ref[...] = _rms(yy, bn_ref[...])


def _gdn(gdn_in, ba, cw, alog, dtb, eb, ea, bones, nw, bn):
    s = gdn_in.shape[0]
    tb = min(s, TB_GDN)
    row = _full((1, GROUP_W))
    return pl.pallas_call(
        _gdn_kernel, grid=(s // tb,),
        in_specs=[_rows(tb, 1024), _rows(tb, 128), _full((CONV_K, 3 * GROUP_W)), _full((1, 128)), _full((1, 128)),
                  _full((128, GROUP_W)), _full((128, GROUP_W)), _full((GROUP_W, GROUP_W)), row, row],
        out_specs=_rows(tb, GROUP_W),
        out_shape=jax.ShapeDtypeStruct((s, GROUP_W), f32),
        scratch_shapes=[pltpu.VMEM((tb + 8, 3 * GROUP_W), f32), pltpu.VMEM((GDN_HEADS, GDN_HD, GDN_HD), f32),
                        pltpu.VMEM((tb, GROUP_W), f32)],
        compiler_params=_params(), name="gated_deltanet")(gdn_in, ba, cw, alog, dtb, eb, ea, bones, nw, bn)


def _out_kernel(h_ref, m0_ref, m1_ref, m2_ref, m3_ref, w_ref, out_ref):
    acc = h_ref[...]
    for j, m in enumerate((m0_ref, m1_ref, m2_ref, m3_ref)):
        acc = acc + jnp.dot(m[...].astype(bf16), w_ref[j * GROUP_W:(j + 1) * GROUP_W, :], preferred_element_type=f32)
    out_ref[...] = acc


def _out_proj(h, mixed, w):
    s = h.shape[0]
    tb = min(s, TB_OUT)
    return pl.pallas_call(
        _out_kernel, grid=(s // tb,),
        in_specs=[_rows(tb, D_MODEL)] + [_rows(tb, GROUP_W)] * 4 + [_full((D_MODEL, D_MODEL))],
        out_specs=_rows(tb, D_MODEL),
        out_shape=jax.ShapeDtypeStruct((s, D_MODEL), f32),
        compiler_params=_params(), name="out_proj")(h, *mixed, w)


def _extract_topk(s, idx_col, n_bad, put):
    for r in range(PEER_TOPK):
        m = jnp.max(s, axis=0, keepdims=True)
        idx = jnp.min(jnp.where(s == m, idx_col, n_bad), axis=0, keepdims=True)
        oh = idx_col == idx
        put(r, m, oh)
        s = jnp.where(oh, -jnp.inf, s)


def _topk_kernel(h_ref, nw_ref, wqt_ref, sk_ref, p1_ref, p2_ref, cidx_ref, qsel_ref,
                 zt_ref, r2_ref, jc_ref, e1_ref, e2_ref, s_scr, rank_scr, val_scr):
    tb = h_ref.shape[0]
    z = _rms(h_ref[...], nw_ref[...])
    zt = z.T.astype(bf16)
    zt_ref[...] = zt
    qt = jnp.dot(wqt_ref[...], zt, preferred_element_type=f32)
    for hp in range(2 * PEER_HEADS):
        s_scr[hp] = _dot(sk_ref[hp], qt[hp * PEER_HALF:(hp + 1) * PEER_HALF, :])

    key_idx = lax.broadcasted_iota(jnp.int32, (PEER_NKEYS, tb), 0).astype(f32)

    def stage1(hp, carry):
        rank = [jnp.full((PEER_NKEYS, tb), float(PEER_TOPK), f32)]

        def put(r, m, oh):
            val_scr[hp, r:r + 1, :] = m
            rank[0] = jnp.where(oh, float(r), rank[0])

        _extract_topk(s_scr[hp], key_idx, float(PEER_NKEYS), put)
        rank_scr[hp] = rank[0]
        return carry

    lax.fori_loop(0, 2 * PEER_HEADS, stage1, 0)

    cidx = jnp.broadcast_to(cidx_ref[...], (PEER_NCAND, tb))
    p1 = p1_ref[...]
    p2 = p2_ref[...]

    def stage2(h, carry):
        v1 = val_scr[2 * h]
        v2 = val_scr[2 * h + 1]
        cand = _dot(p1, v1, hi=True) + _dot(p2, v2, hi=True)
        cand = jnp.where(cidx < 256.0, cand, -jnp.inf)
        sel = [jnp.zeros((PEER_NCAND, tb), f32)]

        def put(r, m, oh):
            sel[0] = jnp.where(oh, 1.0, sel[0])

        _extract_topk(cand, cidx, 1e9, put)
        m1 = v1[0:1, :]
        m2 = v2[0:1, :]
        wgt = sel[0] * _dot(p1, jnp.exp(v1 - m1), hi=True) * _dot(p2, jnp.exp(v2 - m2), hi=True)
        zsum = jnp.sum(wgt, axis=0, keepdims=True)
        cnt = _dot(qsel_ref[...], sel[0], hi=True)
        r1 = rank_scr[2 * h]
        r2 = rank_scr[2 * h + 1]
        jc = jnp.zeros((PEER_NKEYS, tb), f32)
        for i in range(PEER_TOPK):
            jc = jnp.where(r1 == float(i), cnt[i:i + 1, :], jc)
        r2_ref[h] = r2.astype(bf16)
        jc_ref[h] = jc
        e1_ref[h] = jnp.where(r1 < float(PEER_TOPK), jnp.exp(s_scr[2 * h] - m1), 0.0)
        e2_ref[h] = (jnp.where(r2 < float(PEER_TOPK), jnp.exp(s_scr[2 * h + 1] - m2), 0.0) / zsum).astype(bf16)
        return carry

    lax.fori_loop(0, PEER_HEADS, stage2, 0)


def _peer_topk(h, nw, wqt, sk, p1, p2, cidx, qsel):
    s = h.shape[0]
    tb = min(s, TB_TOPK)
    gate_spec = pl.BlockSpec((PEER_HEADS, PEER_NKEYS, tb), lambda i: (0, 0, i))
    gate_shape = jax.ShapeDtypeStruct((PEER_HEADS, PEER_NKEYS, s), f32)
    gate_shape16 = jax.ShapeDtypeStruct((PEER_HEADS, PEER_NKEYS, s), bf16)
    return pl.pallas_call(
        _topk_kernel, grid=(s // tb,),
        in_specs=[_rows(tb, D_MODEL), _full((1, D_MODEL)), _full((D_MODEL, D_MODEL)),
                  _full((2 * PEER_HEADS, PEER_NKEYS, PEER_HALF)), _full((PEER_NCAND, PEER_TOPK)),
                  _full((PEER_NCAND, PEER_TOPK)), _full((PEER_NCAND, 1)), _full((PEER_TOPK, PEER_NCAND))],
        out_specs=[pl.BlockSpec((D_MODEL, tb), lambda i: (0, i)), gate_spec, gate_spec, gate_spec, gate_spec],
        out_shape=[jax.ShapeDtypeStruct((D_MODEL, s), bf16), gate_shape16, gate_shape, gate_shape, gate_shape16],
        scratch_shapes=[pltpu.VMEM((2 * PEER_HEADS, PEER_NKEYS, tb), f32),
                        pltpu.VMEM((2 * PEER_HEADS, PEER_NKEYS, tb), f32),
                        pltpu.VMEM((2 * PEER_HEADS, PEER_TOPK, tb), f32)],
        compiler_params=_params(), name="peer_topk")(h, nw, wqt, sk, p1, p2, cidx, qsel)


def _peer_kernel(zt_ref, h_ref, u_ref, vt_ref, r2_ref, jc_ref, e1_ref, e2_ref, out_ref, acc_ref, act_ref, hh_ref):
    j = pl.program_id(1)
    tb = h_ref.shape[0]
    ne = u_ref.shape[0]
    na = ne // PEER_NKEYS
    n_tiles = pl.num_programs(1) - 2

    @pl.when(j == 0)
    def _():
        acc_ref[...] = jnp.zeros_like(acc_ref)
        act_ref[...] = jnp.zeros_like(act_ref)
        hh_ref[...] = jnp.zeros_like(hh_ref)

    cur = j % 2
    prev = 1 - cur
    tile = jnp.clip(j - 1, 0, n_tiles - 1)
    rows_v = acc_ref.shape[0] // na
    for ai in range(na):
        v0 = ai * rows_v
        acc_ref[v0:v0 + rows_v, :] += jnp.dot(vt_ref[v0:v0 + rows_v, :], hh_ref[cur],
                                               preferred_element_type=f32)
        a = tile * na + ai
        g = jnp.zeros((PEER_NKEYS, tb), bf16)
        for h in range(PEER_HEADS):
            jc = jc_ref[h, pl.ds(a, 1), :].astype(bf16)
            e1 = e1_ref[h, pl.ds(a, 1), :].astype(bf16)
            g = g + jnp.where(r2_ref[h] < jc, e2_ref[h] * e1, jnp.zeros_like(g))
        r0 = ai * PEER_NKEYS
        hh_ref[prev, r0:r0 + PEER_NKEYS, :] = g * jax.nn.gelu(act_ref[prev, r0:r0 + PEER_NKEYS, :].astype(bf16))
        act_ref[cur, r0:r0 + PEER_NKEYS, :] = jnp.dot(u_ref[r0:r0 + PEER_NKEYS, :], zt_ref[...],
                                                      preferred_element_type=f32)

    @pl.when(j == pl.num_programs(1) - 1)
    def _():
        out_ref[...] = h_ref[...] + acc_ref[...].T


def _peer_dense(zt, h, u, vt, r2, jc, e1, e2):
    s = h.shape[0]
    tb = min(s, TB_PEER)
    n_tiles = u.shape[0] // NE_TILE
    gate_spec = pl.BlockSpec((PEER_HEADS, PEER_NKEYS, tb), lambda i, j: (0, 0, i))
    return pl.pallas_call(
        _peer_kernel, grid=(s // tb, n_tiles + 2),
        in_specs=[pl.BlockSpec((D_MODEL, tb), lambda i, j: (0, i)), pl.BlockSpec((tb, D_MODEL), lambda i, j: (i, 0)),
                  pl.BlockSpec((NE_TILE, D_MODEL), lambda i, j: (jnp.minimum(j, n_tiles - 1), 0)),
                  pl.BlockSpec((D_MODEL, NE_TILE), lambda i, j: (0, jnp.clip(j - 2, 0, n_tiles - 1))),
                  gate_spec, gate_spec, gate_spec, gate_spec],
        out_specs=pl.BlockSpec((tb, D_MODEL), lambda i, j: (i, 0)),
        out_shape=jax.ShapeDtypeStruct((s, D_MODEL), f32),
        scratch_shapes=[pltpu.VMEM((D_MODEL, tb), f32), pltpu.VMEM((2, NE_TILE, tb), f32),
                        pltpu.VMEM((2, NE_TILE, tb), bf16)],
        compiler_params=_params(2), name="peer_dense")(zt, h, u, vt, r2, jc, e1, e2)


def _ple_kernel(h_ref, p_ref, nw_ref, wg_ref, wp_ref, fw_ref, out_ref, *, final):
    x = h_ref[...]
    gate = jax.nn.sigmoid(jnp.dot(_rms(x, nw_ref[...]).astype(bf16), wg_ref[...], preferred_element_type=f32))
    y = x + jnp.dot(p_ref[...].astype(bf16), wp_ref[...], preferred_element_type=f32) * gate
    if final:
        y = _rms(y, fw_ref[...])
    out_ref[...] = y


def _ple(h, p, nw, wg, wp, fw, final):
    s = h.shape[0]
    tb = min(s, TB_PLE)
    pw = p.shape[1]
    return pl.pallas_call(
        functools.partial(_ple_kernel, final=final), grid=(s // tb,),
        in_specs=[_rows(tb, D_MODEL), _rows(tb, pw), _full((1, D_MODEL)), _full((D_MODEL, D_MODEL)),
                  _full((pw, D_MODEL)), _full((1, D_MODEL))],
        out_specs=_rows(tb, D_MODEL),
        out_shape=jax.ShapeDtypeStruct((s, D_MODEL), f32),
        compiler_params=_params(), name="ple_gate")(h, p, nw, wg, wp, fw)


def _block_diag(blocks):
    n, r, c = blocks.shape
    return jnp.einsum('nrc,nm->nrmc', blocks, jnp.eye(n, dtype=blocks.dtype)).reshape(n * r, n * c)


def _retention_consts():
    log_gamma = jnp.log(1.0 - 2.0 ** (-5.0 - jnp.arange(RET_HEADS, dtype=f32)))
    idx = jnp.arange(RET_CHUNK, dtype=f32)
    diff = idx[:, None] - idx[None, :]
    causal = diff >= 0
    dmask = jnp.where(causal, jnp.exp(log_gamma[:, None, None] * jnp.where(causal, diff, 0.0)), 0.0)
    q_decay = jnp.exp(log_gamma[:, None] * (idx + 1.0))
    k_decay = jnp.exp(log_gamma[:, None] * (RET_CHUNK - 1.0 - idx))
    chunk_decay = jnp.exp(log_gamma * RET_CHUNK)
    qdec = jnp.repeat(q_decay.T, RET_HD, axis=1)
    kdec = jnp.repeat(k_decay.T, RET_HD, axis=1)
    bmask = _block_diag(jnp.ones((RET_HEADS, RET_HD, RET_HD), f32))
    cdec = _block_diag(jnp.broadcast_to(chunk_decay[:, None, None], (RET_HEADS, RET_HD, RET_HD)))
    return dmask, qdec, kdec, cdec, bmask, bmask


def _s5_discretize(a_re, a_im, b_re, b_im, c_re, c_im, log_dt):
    dt = jnp.exp(log_dt)[:, None]
    mag = jnp.exp(a_re * dt)
    ang = a_im * dt
    ab_re, ab_im = mag * jnp.cos(ang), mag * jnp.sin(ang)
    den = a_re * a_re + a_im * a_im
    p_re, p_im = ab_re - 1.0, ab_im
    f_re = (p_re * a_re + p_im * a_im) / den
    f_im = (p_im * a_re - p_re * a_im) / den
    bb_re = f_re[..., None] * b_re - f_im[..., None] * b_im
    bb_im = f_re[..., None] * b_im + f_im[..., None] * b_re
    abar = jnp.stack([ab_re.reshape(-1), ab_im.reshape(-1)])
    bre = _block_diag(jnp.swapaxes(bb_re, 1, 2))
    bim = _block_diag(jnp.swapaxes(bb_im, 1, 2))
    cre = _block_diag(jnp.swapaxes(c_re, 1, 2))
    cim = _block_diag(jnp.swapaxes(c_im, 1, 2))
    return abar, bre, bim, cre, cim


def _peer_rank_consts():
    pairs = [(i, j) for i in range(PEER_TOPK) for j in range(PEER_TOPK) if (i + 1) * (j + 1) <= PEER_TOPK]
    p1 = np.zeros((PEER_NCAND, PEER_TOPK), np.float32)
    p2 = np.zeros((PEER_NCAND, PEER_TOPK), np.float32)
    cidx = np.full((PEER_NCAND, 1), 1e9, np.float32)
    for c, (i, j) in enumerate(pairs):
        p1[c, i] = 1.0
        p2[c, j] = 1.0
        cidx[c, 0] = PEER_TOPK * i + j
    return jnp.asarray(p1), jnp.asarray(p2), jnp.asarray(cidx), jnp.asarray(p1.T)


def _row(v):
    return v.reshape(1, -1).astype(f32)


def kernel(x, p, positions, mix_norm, w_in, ret_gn, lru_conv_w, lru_conv_b, lru_w_a, lru_b_a, lru_w_x, lru_b_x, lru_lambda, s5_a_re, s5_a_im, s5_b_re, s5_b_im, s5_c_re, s5_c_im, s5_d, s5_log_dt, s5_glu_w, s5_glu_b, gdn_conv_w, gdn_a_log, gdn_dt_bias, gdn_norm, branch_norm, w_out, ffn_norm, peer_wq, peer_subkeys, peer_u, peer_v, ple_norm, ple_wg, ple_wp, final_norm):
    bsz, seq, _ = x.shape
    depth = w_in.shape[0]
    assert bsz == 1
    h = x.reshape(seq, D_MODEL)

    half = RET_HD // 2
    inv_freq = ROPE_BASE ** (-jnp.arange(half, dtype=f32) / half)
    freq = jnp.tile(inv_freq, 128 // half).reshape(1, 128)
    cos, sin = _rope_tables(positions.reshape(seq, 1).astype(f32), freq)

    ret_consts = _retention_consts()
    bones = ret_consts[5]
    p1, p2, cidx, qsel = _peer_rank_consts()
    head_rows = jnp.arange(128)[:, None]
    head_lanes = jnp.arange(GROUP_W)[None, :] // GDN_HD
    eb = (head_rows == head_lanes).astype(f32)
    ea = (head_rows == head_lanes + GDN_HEADS).astype(f32)

    for l in range(depth):
        w = w_in[l]
        w_perm = jnp.concatenate([w[:, 0:2560], w[:, 2568:2824], w[:, 2560:2568],
                                  jnp.zeros((D_MODEL, IN_PAD - 2824), f32)], axis=1).astype(bf16)
        ret_in, lru_in, s5_in, gdn_in, ba_in = _in_proj(h, _row(mix_norm[l]), w_perm)

        y_ret = _retention(ret_in, cos, sin, ret_consts, _row(ret_gn[l]), _row(branch_norm[l, 0]))
        y_lru = _rglru(lru_in, lru_conv_w[l], _row(lru_conv_b[l]), _block_diag(lru_w_a[l]), _row(lru_b_a[l]),
                       _block_diag(lru_w_x[l]), _row(lru_b_x[l]), _row(lru_lambda[l]), _row(branch_norm[l, 1]))
        abar, bre, bim, cre, cim = _s5_discretize(s5_a_re[l], s5_a_im[l], s5_b_re[l], s5_b_im[l],
                                                  s5_c_re[l], s5_c_im[l], s5_log_dt[l])
        y_s5 = _s5(s5_in, abar, bre, bim, cre, cim, _row(s5_d[l]), s5_glu_w[l], _row(s5_glu_b[l]),
                   _row(branch_norm[l, 2]))
        alog = jnp.zeros((1, 128), f32).at[0, GDN_HEADS:2 * GDN_HEADS].set(gdn_a_log[l])
        dtb = jnp.zeros((1, 128), f32).at[0, GDN_HEADS:2 * GDN_HEADS].set(gdn_dt_bias[l])
        y_gdn = _gdn(gdn_in, ba_in, gdn_conv_w[l], alog, dtb, eb, ea, bones,
                     _row(jnp.tile(gdn_norm[l], GDN_HEADS)), _row(branch_norm[l, 3]))

        h = _out_proj(h, (y_ret, y_lru, y_s5, y_gdn), w_out[l].astype(bf16))

        sk = peer_subkeys[l].reshape(2 * PEER_HEADS, PEER_NKEYS, PEER_HALF)
        zt, r2, jc, e1, e2 = _peer_topk(h, _row(ffn_norm[l]), peer_wq[l].T.astype(bf16), sk, p1, p2, cidx, qsel)
        h = _peer_dense(zt, h, peer_u[l].astype(bf16), peer_v[l].T.astype(bf16), r2, jc, e1, e2)

        h = _ple(h, p[l].reshape(seq, -1), _row(ple_norm[l]), ple_wg[l].astype(bf16), ple_wp[l].astype(bf16),
                 _row(final_norm), final=(l == depth - 1))
    return h.reshape(bsz, seq, D_MODEL)
```

```python
import functools
import math

import jax
import jax.numpy as jnp
import numpy as np
from jax import lax
from jax.experimental import pallas as pl
from jax.experimental.pallas import tpu as pltpu

f32 = jnp.float32
bf16 = jnp.bfloat16
_HI = lax.Precision.HIGHEST
_NT = (((1,), (1,)), ((), ()))
_TN = (((0,), (0,)), ((), ()))

EPS = 1e-6
D_MODEL = 1024
GROUP_W = 256
CONV_K = 4
RET_HEADS = 4
RET_HD = 64
RET_CHUNK = 128
ROPE_BASE = 10000.0
LRU_C = 8.0
S5_GROUPS = 16
S5_GW = 16
S5_STATE = 64
S5_N = S5_GROUPS * S5_STATE
GDN_HEADS = 4
GDN_HD = 64
GDN_CHUNK = 64
PEER_HEADS = 8
PEER_NKEYS = 128
PEER_HALF = 64
PEER_TOPK = 16
PEER_NCAND = 64

IN_PAD = 2944
VMEM_LIMIT = 48 * 1024 * 1024

TB_IN = 256
TB_RET = 512
TB_LRU = 512
TB_S5 = 512
TB_GDN = 256
TB_OUT = 512
TB_TOPK = 256
TB_PEER = 512
NE_TILE = 512
TB_PLE = 512


def _dot(a, b, dims=(((1,), (0,)), ((), ())), hi=False):
    if hi:
        return lax.dot_general(a, b, dims, precision=_HI, preferred_element_type=f32)
    return lax.dot_general(a.astype(bf16), b.astype(bf16), dims, preferred_element_type=f32)


def _split(a):
    hi = a.astype(bf16)
    return hi, (a - hi.astype(f32)).astype(bf16)


def _bdot(a, b, dims=(((1,), (0,)), ((), ()))):
    return lax.dot_general(a, b, dims, preferred_element_type=f32)


def _dot3(a, b, dims=(((1,), (0,)), ((), ()))):
    (ah, al), (bh, bl) = a, b
    return _bdot(ah, bh, dims) + (_bdot(ah, bl, dims) + _bdot(al, bh, dims))


def _dot_sel(x, sel):
    xh, xl = _split(x)
    s = sel.astype(bf16)
    return _bdot(xh, s) + _bdot(xl, s)


def _dot_sel_lhs(sel, x):
    xh, xl = _split(x)
    return _bdot(sel, xh) + _bdot(sel, xl)


def _rms(x, w):
    return x * lax.rsqrt(jnp.mean(x * x, axis=-1, keepdims=True) + EPS) * w


def _shift_rows(x, d, fill):
    row = lax.broadcasted_iota(jnp.int32, x.shape, 0)
    return jnp.where(row >= d, pltpu.roll(x, d, 0), fill)


def _params(n_axes=1):
    return pltpu.CompilerParams(dimension_semantics=("arbitrary",) * n_axes, vmem_limit_bytes=VMEM_LIMIT)


def _full(shape):
    n = len(shape)
    return pl.BlockSpec(shape, lambda *_: (0,) * n)


def _rows(tb, w):
    return pl.BlockSpec((tb, w), lambda i: (i, 0))


def _rope_kernel(pos_ref, freq_ref, cos_ref, sin_ref):
    ang = pos_ref[...] * freq_ref[...]
    cos_ref[...] = jnp.cos(ang)
    sin_ref[...] = jnp.sin(ang)


def _rope_tables(pos, freq):
    s = pos.shape[0]
    tb = min(s, 1024)
    return pl.pallas_call(
        _rope_kernel, grid=(s // tb,),
        in_specs=[_rows(tb, 1), _full((1, 128))],
        out_specs=[_rows(tb, 128), _rows(tb, 128)],
        out_shape=[jax.ShapeDtypeStruct((s, 128), f32)] * 2,
        compiler_params=_params(), name="rope_tables")(pos, freq)


def _in_kernel(h_ref, nw_ref, w_ref, ret_ref, lru_ref, s5_ref, gdn_ref, ba_ref):
    z = _rms(h_ref[...], nw_ref[...]).astype(bf16)
    off = 0
    for ref in (ret_ref, lru_ref, s5_ref, gdn_ref, ba_ref):
        w = ref.shape[1]
        ref[...] = jnp.dot(z, w_ref[:, off:off + w], preferred_element_type=f32)
        off += w


def _in_proj(h, nw, w):
    s = h.shape[0]
    tb = min(s, TB_IN)
    widths = (1024, 512, 256, 1024, 128)
    return pl.pallas_call(
        _in_kernel, grid=(s // tb,),
        in_specs=[_rows(tb, D_MODEL), _full((1, D_MODEL)), _full((D_MODEL, IN_PAD))],
        out_specs=[_rows(tb, w_) for w_ in widths],
        out_shape=[jax.ShapeDtypeStruct((s, w_), f32) for w_ in widths],
        compiler_params=_params(), name="in_proj")(h, nw, w)


def _ret_kernel(x_ref, cos_ref, sin_ref, dmask_ref, qdec_ref, kdec_ref, cdec_ref, bmask_ref, bones_ref,
                gn_ref, bn_ref, out_ref, state_ref, o_scr):
    tb = x_ref.shape[0]

    @pl.when(pl.program_id(0) == 0)
    def _():
        state_ref[...] = jnp.zeros_like(state_ref)

    lane = lax.broadcasted_iota(jnp.int32, (1, GROUP_W), 1)
    first = (lane % RET_HD) < (RET_HD // 2)
    cosf = jnp.concatenate([cos_ref[...], cos_ref[...]], axis=1)
    sinf = jnp.concatenate([sin_ref[...], sin_ref[...]], axis=1) * jnp.where(first, -1.0, 1.0)

    def rot(t):
        partner = jnp.where(first, pltpu.roll(t, GROUP_W - RET_HD // 2, 1), pltpu.roll(t, RET_HD // 2, 1))
        return t * cosf + partner * sinf

    q = rot(x_ref[:, 0:256]) * (RET_HD ** -0.5)
    k = rot(x_ref[:, 256:512])
    v = x_ref[:, 512:768]
    qdec = qdec_ref[...]
    kdec = kdec_ref[...]
    for c in range(tb // RET_CHUNK):
        r0 = c * RET_CHUNK
        qc, kc, vc = q[r0:r0 + RET_CHUNK], k[r0:r0 + RET_CHUNK], v[r0:r0 + RET_CHUNK]
        st = state_ref[...]
        o = _dot(qc, st) * qdec
        for h in range(RET_HEADS):
            hm = (lane // RET_HD) == h
            sc = _dot(jnp.where(hm, qc, 0.0), kc, _NT) * dmask_ref[h]
            o = o + _dot(sc, jnp.where(hm, vc, 0.0))
        kv = _dot(kc * kdec, vc, _TN)
        state_ref[...] = st * cdec_ref[...] + kv * bmask_ref[...]
        o_scr[r0:r0 + RET_CHUNK, :] = o
    o = o_scr[...]
    bones = bones_ref[...]
    mu = _dot(o, bones, hi=True) * (1.0 / RET_HD)
    d = o - mu
    var = _dot(d * d, bones, hi=True) * (1.0 / RET_HD)
    on = d * lax.rsqrt(var + EPS) * gn_ref[...]
    g = x_ref[:, 768:1024]
    y = g * jax.nn.sigmoid(g) * on
    out_ref[...] = _rms(y, bn_ref[...])


def _retention(ret_in, cos, sin, consts, gn, bn):
    s = ret_in.shape[0]
    tb = min(s, TB_RET)
    dmask, qdec, kdec, cdec, bmask, bones = consts
    return pl.pallas_call(
        _ret_kernel, grid=(s // tb,),
        in_specs=[_rows(tb, 1024), _rows(tb, 128), _rows(tb, 128), _full(dmask.shape), _full(qdec.shape),
                  _full(kdec.shape), _full(cdec.shape), _full(bmask.shape), _full(bones.shape),
                  _full((1, GROUP_W)), _full((1, GROUP_W))],
        out_specs=_rows(tb, GROUP_W),
        out_shape=jax.ShapeDtypeStruct((s, GROUP_W), f32),
        scratch_shapes=[pltpu.VMEM((GROUP_W, GROUP_W), f32), pltpu.VMEM((tb, GROUP_W), f32)],
        compiler_params=_params(), name="retention")(ret_in, cos, sin, dmask, qdec, kdec, cdec, bmask, bones, gn, bn)


def _lru_kernel(x_ref, cw_ref, cb_ref, wa_ref, ba_ref, wx_ref, bx_ref, lam_ref, bn_ref, out_ref, xbuf, carry_ref):
    tb = x_ref.shape[0]

    @pl.when(pl.program_id(0) == 0)
    def _():
        xbuf[0:8, :] = jnp.zeros((8, GROUP_W), f32)
        carry_ref[...] = jnp.zeros_like(carry_ref)

    x = x_ref[:, 256:512]
    xbuf[8:8 + tb, :] = x
    cw = cw_ref[...]
    xb = (cw[3:4] * x + cw[2:3] * xbuf[7:7 + tb, :] + cw[1:2] * xbuf[6:6 + tb, :]
          + cw[0:1] * xbuf[5:5 + tb, :]) + cb_ref[...]
    xbuf[0:8, :] = xbuf[tb:tb + 8, :]
    r = jax.nn.sigmoid(_dot(xb, wa_ref[...], hi=True) + ba_ref[...])
    i = jax.nn.sigmoid(_dot(xb, wx_ref[...], hi=True) + bx_ref[...])
    log_a = -LRU_C * r * jax.nn.softplus(-lam_ref[...])
    a = jnp.exp(log_a)
    y2 = 2.0 * log_a
    one_minus_a2 = -jnp.tanh(0.5 * y2) * (jnp.exp(y2) + 1.0)
    b = jnp.sqrt(one_minus_a2) * (i * xb)
    d = 1
    while d < tb:
        b = a * _shift_rows(b, d, 0.0) + b
        a = a * _shift_rows(a, d, 1.0)
        d *= 2
    h = b + a * carry_ref[...]
    carry_ref[...] = h[tb - 1:tb, :]
    y = jax.nn.gelu(x_ref[:, 0:256]) * h
    out_ref[...] = _rms(y, bn_ref[...])


def _rglru(lru_in, cw, cb, wa, ba, wx, bx, lam, bn):
    s = lru_in.shape[0]
    tb = min(s, TB_LRU)
    row = _full((1, GROUP_W))
    sq = _full((GROUP_W, GROUP_W))
    return pl.pallas_call(
        _lru_kernel, grid=(s // tb,),
        in_specs=[_rows(tb, 512), _full((CONV_K, GROUP_W)), row, sq, row, sq, row, row, row],
        out_specs=_rows(tb, GROUP_W),
        out_shape=jax.ShapeDtypeStruct((s, GROUP_W), f32),
        scratch_shapes=[pltpu.VMEM((tb + 8, GROUP_W), f32), pltpu.VMEM((1, GROUP_W), f32)],
        compiler_params=_params(), name="rglru")(lru_in, cw, cb, wa, ba, wx, bx, lam, bn)


def _s5_kernel(u_ref, a_ref, bre_ref, bim_ref, cre_ref, cim_ref, d_ref, gw_ref, gb_ref, bn_ref, out_ref, carry_ref):
    tb = u_ref.shape[0]

    @pl.when(pl.program_id(0) == 0)
    def _():
        carry_ref[...] = jnp.zeros_like(carry_ref)

    u = u_ref[...]
    ar = a_ref[0:1, :]
    ai = a_ref[1:2, :]
    xr = _dot(u, bre_ref[...])
    xi = _dot(u, bim_ref[...])
    cr = carry_ref[0:1, :]
    ci = carry_ref[1:2, :]
    row = lax.broadcasted_iota(jnp.int32, xr.shape, 0)
    xr = jnp.where(row == 0, xr + (ar * cr - ai * ci), xr)
    xi = jnp.where(row == 0, xi + (ar * ci + ai * cr), xi)
    d = 1
    while d < tb:
        sr = _shift_rows(xr, d, 0.0)
        si = _shift_rows(xi, d, 0.0)
        xr = xr + (ar * sr - ai * si)
        xi = xi + (ar * si + ai * sr)
        ar, ai = ar * ar - ai * ai, 2.0 * ar * ai
        d *= 2
    carry_ref[0:1, :] = xr[tb - 1:tb, :]
    carry_ref[1:2, :] = xi[tb - 1:tb, :]
    y = _dot(xr, cre_ref[...]) - _dot(xi, cim_ref[...]) + d_ref[...] * u
    zz = _dot(jax.nn.gelu(y), gw_ref[...]) + gb_ref[...]
    o = zz[:, 0:GROUP_W] * jax.nn.sigmoid(zz[:, GROUP_W:2 * GROUP_W])
    out_ref[...] = _rms(o, bn_ref[...])


def _s5(s5_in, a, bre, bim, cre, cim, dd, gw, gb, bn):
    s = s5_in.shape[0]
    tb = min(s, TB_S5)
    row = _full((1, GROUP_W))
    return pl.pallas_call(
        _s5_kernel, grid=(s // tb,),
        in_specs=[_rows(tb, GROUP_W), _full((2, S5_N)), _full((GROUP_W, S5_N)), _full((GROUP_W, S5_N)),
                  _full((S5_N, GROUP_W)), _full((S5_N, GROUP_W)), row, _full((GROUP_W, 2 * GROUP_W)),
                  _full((1, 2 * GROUP_W)), row],
        out_specs=_rows(tb, GROUP_W),
        out_shape=jax.ShapeDtypeStruct((s, GROUP_W), f32),
        scratch_shapes=[pltpu.VMEM((2, S5_N), f32)],
        compiler_params=_params(), name="s5")(s5_in, a, bre, bim, cre, cim, dd, gw, gb, bn)


def _gdn_kernel(x_ref, ba_ref, cw_ref, alog_ref, dtb_ref, eb_ref, ea_ref, bones_ref, nw_ref, bn_ref, out_ref,
                xbuf, state_ref, o_scr):
    tb = x_ref.shape[0]
    c_ = GDN_CHUNK

    @pl.when(pl.program_id(0) == 0)
    def _():
        xbuf[0:8, :] = jnp.zeros((8, 3 * GROUP_W), f32)
        state_ref[...] = jnp.zeros_like(state_ref)

    x = x_ref[:, 0:768]
    xbuf[8:8 + tb, :] = x
    cw = cw_ref[...]
    y = cw[3:4] * x + cw[2:3] * xbuf[7:7 + tb, :] + cw[1:2] * xbuf[6:6 + tb, :] + cw[0:1] * xbuf[5:5 + tb, :]
    xbuf[0:8, :] = xbuf[tb:tb + 8, :]
    y = y * jax.nn.sigmoid(y)
    bones = bones_ref[...]
    q = y[:, 0:256]
    k = y[:, 256:512]
    v = y[:, 512:768]
    q = q * lax.rsqrt(_dot_sel(q * q, bones) + EPS) * (GDN_HD ** -0.5)
    k = k * lax.rsqrt(_dot_sel(k * k, bones) + EPS)
    ba = ba_ref[...]
    beta_b = _dot_sel(jax.nn.sigmoid(ba), eb_ref[...])
    la = -jnp.exp(alog_ref[...]) * jax.nn.softplus(ba + dtb_ref[...])
    la_b = _dot_sel(la, ea_ref[...])

    ri = lax.broadcasted_iota(jnp.int32, (c_, c_), 0)
    ci = lax.broadcasted_iota(jnp.int32, (c_, c_), 1)
    incl = ri >= ci
    strict = ri > ci
    eye = ri == ci
    ltri = incl.astype(bf16)
    ri4 = lax.broadcasted_iota(jnp.int32, (c_, GROUP_W), 0)
    ci4 = lax.broadcasted_iota(jnp.int32, (c_, GROUP_W), 1) % GDN_HD
    lev_masks = []
    b = 1
    while b < c_:
        lev_masks.append(((ri // (2 * b)) == (ci // (2 * b))) & (((ri // b) % 2) == 1) & (((ci // b) % 2) == 0))
        b *= 2

    n_chunks = tb // c_
    items = [(c, h) for c in range(n_chunks) for h in range(GDN_HEADS)]
    per_chunk = []
    for c in range(n_chunks):
        r0 = c * c_
        la_c = la_b[r0:r0 + c_, :]
        gcum = _dot_sel_lhs(ltri, la_c)
        dmat = _dot_sel_lhs(ltri, jnp.where(ri4 > ci4, la_c, 0.0))
        lm = jnp.where(ri4 >= ci4, jnp.exp(dmat), 0.0)
        eg = jnp.exp(gcum)
        gl = gcum[c_ - 1:c_, :]
        qc, kc, vc, bc = q[r0:r0 + c_, :], k[r0:r0 + c_, :], v[r0:r0 + c_, :], beta_b[r0:r0 + c_, :]
        kb = kc * bc
        per_chunk.append(dict(lm=lm, k=kc, q=qc, kb=kb, vb=vc * bc, kbe=kb * eg, qd=qc * eg,
                              kd=kc * jnp.exp(gl - gcum), cd=jnp.exp(gl)))
    lmat_f, lmat, attn = {}, {}, {}
    for (c, h) in items:
        cs = slice(h * GDN_HD, (h + 1) * GDN_HD)
        pc = per_chunk[c]
        lm = pc['lm'][:, cs]
        kh = pc['k'][:, cs]
        lmat_f[c, h] = jnp.where(strict, _dot(pc['kb'][:, cs], kh, _NT) * lm, 0.0)
        lmat[c, h] = _split(lmat_f[c, h])
        attn[c, h] = jnp.where(incl, _dot(pc['q'][:, cs], kh, _NT) * lm, 0.0)
    tinv = {n: _split(jnp.where(eye, 1.0, 0.0) - jnp.where(lev_masks[0], lmat_f[n], 0.0)) for n in items}
    for m in lev_masks[1:]:
        xs = {n: _split(_dot3((jnp.where(m, lmat[n][0], jnp.zeros_like(lmat[n][0])),
                               jnp.where(m, lmat[n][1], jnp.zeros_like(lmat[n][1]))), tinv[n])) for n in items}
        tinv = {n: _split(tinv[n][0].astype(f32) + tinv[n][1].astype(f32) - _dot3(tinv[n], xs[n])) for n in items}
    u, w, mm, nn, pp, rr = {}, {}, {}, {}, {}, {}
    for (c, h) in items:
        cs = slice(h * GDN_HD, (h + 1) * GDN_HD)
        pc = per_chunk[c]
        u[c, h] = _dot3(tinv[c, h], _split(pc['vb'][:, cs]))
        w[c, h] = _dot3(tinv[c, h], _split(pc['kbe'][:, cs]))
    for (c, h) in items:
        cs = slice(h * GDN_HD, (h + 1) * GDN_HD)
        pc = per_chunk[c]
        kd = _split(pc['kd'][:, cs])
        mm[c, h] = _split(jnp.where(eye, pc['cd'][:, cs], 0.0) - _dot3(kd, _split(w[c, h]), _TN))
        nn[c, h] = _dot3(kd, _split(u[c, h]), _TN)
        pp[c, h] = _dot(attn[c, h], u[c, h])
        rr[c, h] = _split(pc['qd'][:, cs] - _dot(attn[c, h], w[c, h]))
    for c in range(n_chunks):
        r0 = c * c_
        for h in range(GDN_HEADS):
            cs = slice(h * GDN_HD, (h + 1) * GDN_HD)
            st = _split(state_ref[h])
            o_scr[r0:r0 + c_, cs] = pp[c, h] + _dot3(rr[c, h], st)
            state_ref[h] = _dot3(mm[c, h], st) + nn[c, h]
    o = o_scr[...]
    on = o * lax.rsqrt(_dot_sel(o * o, bones) * (1.0 / GDN_HD) + EPS) * nw_ref[...]
    g = x_ref[:, 768:1024]
    yy = on * (g * jax.nn.sigmoid(g))
    out_ref[...] = _rms(yy, bn_ref[...])


def _gdn(gdn_in, ba, cw, alog, dtb, eb, ea, bones, nw, bn):
    s = gdn_in.shape[0]
    tb = min(s, TB_GDN)
    row = _full((1, GROUP_W))
    return pl.pallas_call(
        _gdn_kernel, grid=(s // tb,),
        in_specs=[_rows(tb, 1024), _rows(tb, 128), _full((CONV_K, 3 * GROUP_W)), _full((1, 128)), _full((1, 128)),
                  _full((128, GROUP_W)), _full((128, GROUP_W)), _full((GROUP_W, GROUP_W)), row, row],
        out_specs=_rows(tb, GROUP_W),
        out_shape=jax.ShapeDtypeStruct((s, GROUP_W), f32),
        scratch_shapes=[pltpu.VMEM((tb + 8, 3 * GROUP_W), f32), pltpu.VMEM((GDN_HEADS, GDN_HD, GDN_HD), f32),
                        pltpu.VMEM((tb, GROUP_W), f32)],
        compiler_params=_params(), name="gated_deltanet")(gdn_in, ba, cw, alog, dtb, eb, ea, bones, nw, bn)


def _out_kernel(h_ref, m0_ref, m1_ref, m2_ref, m3_ref, w_ref, out_ref):
    acc = h_ref[...]
    for j, m in enumerate((m0_ref, m1_ref, m2_ref, m3_ref)):
        acc = acc + jnp.dot(m[...].astype(bf16), w_ref[j * GROUP_W:(j + 1) * GROUP_W, :], preferred_element_type=f32)
    out_ref[...] = acc


def _out_proj(h, mixed, w):
    s = h.shape[0]
    tb = min(s, TB_OUT)
    return pl.pallas_call(
        _out_kernel, grid=(s // tb,),
        in_specs=[_rows(tb, D_MODEL)] + [_rows(tb, GROUP_W)] * 4 + [_full((D_MODEL, D_MODEL))],
        out_specs=_rows(tb, D_MODEL),
        out_shape=jax.ShapeDtypeStruct((s, D_MODEL), f32),
        compiler_params=_params(), name="out_proj")(h, *mixed, w)


def _extract_topk(s, idx_col, n_bad, put):
    for r in range(PEER_TOPK):
        m = jnp.max(s, axis=0, keepdims=True)
        idx = jnp.min(jnp.where(s == m, idx_col, n_bad), axis=0, keepdims=True)
        oh = idx_col == idx
        put(r, m, oh)
        s = jnp.where(oh, -jnp.inf, s)


def _topk_kernel(h_ref, nw_ref, wqt_ref, sk_ref, p1_ref, p2_ref, cidx_ref, qsel_ref,
                 zt_ref, r2_ref, jc_ref, e1_ref, e2_ref, s_scr, rank_scr, val_scr):
    tb = h_ref.shape[0]
    z = _rms(h_ref[...], nw_ref[...])
    zt = z.T.astype(bf16)
    zt_ref[...] = zt
    qt = jnp.dot(wqt_ref[...], zt, preferred_element_type=f32)
    for hp in range(2 * PEER_HEADS):
        s_scr[hp] = _dot(sk_ref[hp], qt[hp * PEER_HALF:(hp + 1) * PEER_HALF, :])

    key_idx = lax.broadcasted_iota(jnp.int32, (PEER_NKEYS, tb), 0).astype(f32)

    def stage1(hp, carry):
        rank = [jnp.full((PEER_NKEYS, tb), float(PEER_TOPK), f32)]

        def put(r, m, oh):
            val_scr[hp, r:r + 1, :] = m
            rank[0] = jnp.where(oh, float(r), rank[0])

        _extract_topk(s_scr[hp], key_idx, float(PEER_NKEYS), put)
        rank_scr[hp] = rank[0]
        return carry

    lax.fori_loop(0, 2 * PEER_HEADS, stage1, 0)

    cidx = jnp.broadcast_to(cidx_ref[...], (PEER_NCAND, tb))
    p1 = p1_ref[...]
    p2 = p2_ref[...]

    def stage2(h, carry):
        v1 = val_scr[2 * h]
        v2 = val_scr[2 * h + 1]
        cand = _dot(p1, v1, hi=True) + _dot(p2, v2, hi=True)
        cand = jnp.where(cidx < 256.0, cand, -jnp.inf)
        sel = [jnp.zeros((PEER_NCAND, tb), f32)]

        def put(r, m, oh):
            sel[0] = jnp.where(oh, 1.0, sel[0])

        _extract_topk(cand, cidx, 1e9, put)
        m1 = v1[0:1, :]
        m2 = v2[0:1, :]
        wgt = sel[0] * _dot(p1, jnp.exp(v1 - m1), hi=True) * _dot(p2, jnp.exp(v2 - m2), hi=True)
        zsum = jnp.sum(wgt, axis=0, keepdims=True)
        cnt = _dot(qsel_ref[...], sel[0], hi=True)
        r1 = rank_scr[2 * h]
        r2 = rank_scr[2 * h + 1]
        jc = jnp.zeros((PEER_NKEYS, tb), f32)
        for i in range(PEER_TOPK):
            jc = jnp.where(r1 == float(i), cnt[i:i + 1, :], jc)
        r2_ref[h] = r2.astype(bf16)
        jc_ref[h] = jc
        e1_ref[h] = jnp.where(r1 < float(PEER_TOPK), jnp.exp(s_scr[2 * h] - m1), 0.0)
        e2_ref[h] = (jnp.where(r2 < float(PEER_TOPK), jnp.exp(s_scr[2 * h + 1] - m2), 0.0) / zsum).astype(bf16)
        return carry

    lax.fori_loop(0, PEER_HEADS, stage2, 0)


def _peer_topk(h, nw, wqt, sk, p1, p2, cidx, qsel):
    s = h.shape[0]
    tb = min(s, TB_TOPK)
    gate_spec = pl.BlockSpec((PEER_HEADS, PEER_NKEYS, tb), lambda i: (0, 0, i))
    gate_shape = jax.ShapeDtypeStruct((PEER_HEADS, PEER_NKEYS, s), f32)
    gate_shape16 = jax.ShapeDtypeStruct((PEER_HEADS, PEER_NKEYS, s), bf16)
    return pl.pallas_call(
        _topk_kernel, grid=(s // tb,),
        in_specs=[_rows(tb, D_MODEL), _full((1, D_MODEL)), _full((D_MODEL, D_MODEL)),
                  _full((2 * PEER_HEADS, PEER_NKEYS, PEER_HALF)), _full((PEER_NCAND, PEER_TOPK)),
                  _full((PEER_NCAND, PEER_TOPK)), _full((PEER_NCAND, 1)), _full((PEER_TOPK, PEER_NCAND))],
        out_specs=[pl.BlockSpec((D_MODEL, tb), lambda i: (0, i)), gate_spec, gate_spec, gate_spec, gate_spec],
        out_shape=[jax.ShapeDtypeStruct((D_MODEL, s), bf16), gate_shape16, gate_shape, gate_shape, gate_shape16],
        scratch_shapes=[pltpu.VMEM((2 * PEER_HEADS, PEER_NKEYS, tb), f32),
                        pltpu.VMEM((2 * PEER_HEADS, PEER_NKEYS, tb), f32),
                        pltpu.VMEM((2 * PEER_HEADS, PEER_TOPK, tb), f32)],
        compiler_params=_params(), name="peer_topk")(h, nw, wqt, sk, p1, p2, cidx, qsel)


def _peer_kernel(zt_ref, h_ref, u_ref, vt_ref, r2_ref, jc_ref, e1_ref, e2_ref, out_ref, acc_ref, act_ref, hh_ref):
    j = pl.program_id(1)
    tb = h_ref.shape[0]
    ne = u_ref.shape[0]
    na = ne // PEER_NKEYS
    n_tiles = pl.num_programs(1) - 2

    @pl.when(j == 0)
    def _():
        acc_ref[...] = jnp.zeros_like(acc_ref)
        act_ref[...] = jnp.zeros_like(act_ref)
        hh_ref[...] = jnp.zeros_like(hh_ref)

    cur = j % 2
    prev = 1 - cur
    tile = jnp.clip(j - 1, 0, n_tiles - 1)
    rows_v = acc_ref.shape[0] // na
    for ai in range(na):
        v0 = ai * rows_v
        acc_ref[v0:v0 + rows_v, :] += jnp.dot(vt_ref[v0:v0 + rows_v, :], hh_ref[cur],
                                               preferred_element_type=f32)
        a = tile * na + ai
        g = jnp.zeros((PEER_NKEYS, tb), bf16)
        for h in range(PEER_HEADS):
            jc = jc_ref[h, pl.ds(a, 1), :].astype(bf16)
            e1 = e1_ref[h, pl.ds(a, 1), :].astype(bf16)
            g = g + jnp.where(r2_ref[h] < jc, e2_ref[h] * e1, jnp.zeros_like(g))
        r0 = ai * PEER_NKEYS
        hh_ref[prev, r0:r0 + PEER_NKEYS, :] = g * jax.nn.gelu(act_ref[prev, r0:r0 + PEER_NKEYS, :].astype(bf16))
        act_ref[cur, r0:r0 + PEER_NKEYS, :] = jnp.dot(u_ref[r0:r0 + PEER_NKEYS, :], zt_ref[...],
                                                      preferred_element_type=f32)

    @pl.when(j == pl.num_programs(1) - 1)
    def _():
        out_ref[...] = h_ref[...] + acc_ref[...].T


def _peer_dense(zt, h, u, vt, r2, jc, e1, e2):
    s = h.shape[0]
    tb = min(s, TB_PEER)
    n_tiles = u.shape[0] // NE_TILE
    gate_spec = pl.BlockSpec((PEER_HEADS, PEER_NKEYS, tb), lambda i, j: (0, 0, i))
    return pl.pallas_call(
        _peer_kernel, grid=(s // tb, n_tiles + 2),
        in_specs=[pl.BlockSpec((D_MODEL, tb), lambda i, j: (0, i)), pl.BlockSpec((tb, D_MODEL), lambda i, j: (i, 0)),
                  pl.BlockSpec((NE_TILE, D_MODEL), lambda i, j: (jnp.minimum(j, n_tiles - 1), 0)),
                  pl.BlockSpec((D_MODEL, NE_TILE), lambda i, j: (0, jnp.clip(j - 2, 0, n_tiles - 1))),
                  gate_spec, gate_spec, gate_spec, gate_spec],
        out_specs=pl.BlockSpec((tb, D_MODEL), lambda i, j: (i, 0)),
        out_shape=jax.ShapeDtypeStruct((s, D_MODEL), f32),
        scratch_shapes=[pltpu.VMEM((D_MODEL, tb), f32), pltpu.VMEM((2, NE_TILE, tb), f32),
                        pltpu.VMEM((2, NE_TILE, tb), bf16)],
        compiler_params=_params(2), name="peer_dense")(zt, h, u, vt, r2, jc, e1, e2)


def _ple_kernel(h_ref, p_ref, nw_ref, wg_ref, wp_ref, fw_ref, out_ref, *, final):
    x = h_ref[...]
    gate = jax.nn.sigmoid(jnp.dot(_rms(x, nw_ref[...]).astype(bf16), wg_ref[...], preferred_element_type=f32))
    y = x + jnp.dot(p_ref[...].astype(bf16), wp_ref[...], preferred_element_type=f32) * gate
    if final:
        y = _rms(y, fw_ref[...])
    out_ref[...] = y


def _ple(h, p, nw, wg, wp, fw, final):
    s = h.shape[0]
    tb = min(s, TB_PLE)
    pw = p.shape[1]
    return pl.pallas_call(
        functools.partial(_ple_kernel, final=final), grid=(s // tb,),
        in_specs=[_rows(tb, D_MODEL), _rows(tb, pw), _full((1, D_MODEL)), _full((D_MODEL, D_MODEL)),
                  _full((pw, D_MODEL)), _full((1, D_MODEL))],
        out_specs=_rows(tb, D_MODEL),
        out_shape=jax.ShapeDtypeStruct((s, D_MODEL), f32),
        compiler_params=_params(), name="ple_gate")(h, p, nw, wg, wp, fw)


def _block_diag(blocks):
    n, r, c = blocks.shape
    return jnp.einsum('nrc,nm->nrmc', blocks, jnp.eye(n, dtype=blocks.dtype)).reshape(n * r, n * c)


def _retention_consts():
    log_gamma = jnp.log(1.0 - 2.0 ** (-5.0 - jnp.arange(RET_HEADS, dtype=f32)))
    idx = jnp.arange(RET_CHUNK, dtype=f32)
    diff = idx[:, None] - idx[None, :]
    causal = diff >= 0
    dmask = jnp.where(causal, jnp.exp(log_gamma[:, None, None] * jnp.where(causal, diff, 0.0)), 0.0)
    q_decay = jnp.exp(log_gamma[:, None] * (idx + 1.0))
    k_decay = jnp.exp(log_gamma[:, None] * (RET_CHUNK - 1.0 - idx))
    chunk_decay = jnp.exp(log_gamma * RET_CHUNK)
    qdec = jnp.repeat(q_decay.T, RET_HD, axis=1)
    kdec = jnp.repeat(k_decay.T, RET_HD, axis=1)
    bmask = _block_diag(jnp.ones((RET_HEADS, RET_HD, RET_HD), f32))
    cdec = _block_diag(jnp.broadcast_to(chunk_decay[:, None, None], (RET_HEADS, RET_HD, RET_HD)))
    return dmask, qdec, kdec, cdec, bmask, bmask


def _s5_discretize(a_re, a_im, b_re, b_im, c_re, c_im, log_dt):
    dt = jnp.exp(log_dt)[:, None]
    mag = jnp.exp(a_re * dt)
    ang = a_im * dt
    ab_re, ab_im = mag * jnp.cos(ang), mag * jnp.sin(ang)
    den = a_re * a_re + a_im * a_im
    p_re, p_im = ab_re - 1.0, ab_im
    f_re = (p_re * a_re + p_im * a_im) / den
    f_im = (p_im * a_re - p_re * a_im) / den
    bb_re = f_re[..., None] * b_re - f_im[..., None] * b_im
    bb_im = f_re[..., None] * b_im + f_im[..., None] * b_re
    abar = jnp.stack([ab_re.reshape(-1), ab_im.reshape(-1)])
    bre = _block_diag(jnp.swapaxes(bb_re, 1, 2))
    bim = _block_diag(jnp.swapaxes(bb_im, 1, 2))
    cre = _block_diag(jnp.swapaxes(c_re, 1, 2))
    cim = _block_diag(jnp.swapaxes(c_im, 1, 2))
    return abar, bre, bim, cre, cim


def _peer_rank_consts():
    pairs = [(i, j) for i in range(PEER_TOPK) for j in range(PEER_TOPK) if (i + 1) * (j + 1) <= PEER_TOPK]
    p1 = np.zeros((PEER_NCAND, PEER_TOPK), np.float32)
    p2 = np.zeros((PEER_NCAND, PEER_TOPK), np.float32)
    cidx = np.full((PEER_NCAND, 1), 1e9, np.float32)
    for c, (i, j) in enumerate(pairs):
        p1[c, i] = 1.0
        p2[c, j] = 1.0
        cidx[c, 0] = PEER_TOPK * i + j
    return jnp.asarray(p1), jnp.asarray(p2), jnp.asarray(cidx), jnp.asarray(p1.T)


def _row(v):
    return v.reshape(1, -1).astype(f32)


def kernel(x, p, positions, mix_norm, w_in, ret_gn, lru_conv_w, lru_conv_b, lru_w_a, lru_b_a, lru_w_x, lru_b_x, lru_lambda, s5_a_re, s5_a_im, s5_b_re, s5_b_im, s5_c_re, s5_c_im, s5_d, s5_log_dt, s5_glu_w, s5_glu_b, gdn_conv_w, gdn_a_log, gdn_dt_bias, gdn_norm, branch_norm, w_out, ffn_norm, peer_wq, peer_subkeys, peer_u, peer_v, ple_norm, ple_wg, ple_wp, final_norm):
    bsz, seq, _ = x.shape
    depth = w_in.shape[0]
    assert bsz == 1
    h = x.reshape(seq, D_MODEL)

    half = RET_HD // 2
    inv_freq = ROPE_BASE ** (-jnp.arange(half, dtype=f32) / half)
    freq = jnp.tile(inv_freq, 128 // half).reshape(1, 128)
    cos, sin = _rope_tables(positions.reshape(seq, 1).astype(f32), freq)

    ret_consts = _retention_consts()
    bones = ret_consts[5]
    p1, p2, cidx, qsel = _peer_rank_consts()
    head_rows = jnp.arange(128)[:, None]
    head_lanes = jnp.arange(GROUP_W)[None, :] // GDN_HD
    eb = (head_rows == head_lanes).astype(f32)
    ea = (head_rows == head_lanes + GDN_HEADS).astype(f32)

    for l in range(depth):
        w = w_in[l]
        w_perm = jnp.concatenate([w[:, 0:2560], w[:, 2568:2824], w[:, 2560:2568],
                                  jnp.zeros((D_MODEL, IN_PAD - 2824), f32)], axis=1).astype(bf16)
        ret_in, lru_in, s5_in, gdn_in, ba_in = _in_proj(h, _row(mix_norm[l]), w_perm)

        y_ret = _retention(ret_in, cos, sin, ret_consts, _row(ret_gn[l]), _row(branch_norm[l, 0]))
        y_lru = _rglru(lru_in, lru_conv_w[l], _row(lru_conv_b[l]), _block_diag(lru_w_a[l]), _row(lru_b_a[l]),
                       _block_diag(lru_w_x[l]), _row(lru_b_x[l]), _row(lru_lambda[l]), _row(branch_norm[l, 1]))
        abar, bre, bim, cre, cim = _s5_discretize(s5_a_re[l], s5_a_im[l], s5_b_re[l], s5_b_im[l],
                                                  s5_c_re[l], s5_c_im[l], s5_log_dt[l])
        y_s5 = _s5(s5_in, abar, bre, bim, cre, cim, _row(s5_d[l]), s5_glu_w[l], _row(s5_glu_b[l]),
                   _row(branch_norm[l, 2]))
        alog = jnp.zeros((1, 128), f32).at[0, GDN_HEADS:2 * GDN_HEADS].set(gdn_a_log[l])
        dtb = jnp.zeros((1, 128), f32).at[0, GDN_HEADS:2 * GDN_HEADS].set(gdn_dt_bias[l])
        y_gdn = _gdn(gdn_in, ba_in, gdn_conv_w[l], alog, dtb, eb, ea, bones,
                     _row(jnp.tile(gdn_norm[l], GDN_HEADS)), _row(branch_norm[l, 3]))

        h = _out_proj(h, (y_ret, y_lru, y_s5, y_gdn), w_out[l].astype(bf16))

        sk = peer_subkeys[l].reshape(2 * PEER_HEADS, PEER_NKEYS, PEER_HALF)
        zt, r2, jc, e1, e2 = _peer_topk(h, _row(ffn_norm[l]), peer_wq[l].T.astype(bf16), sk, p1, p2, cidx, qsel)
        h = _peer_dense(zt, h, peer_u[l].astype(bf16), peer_v[l].T.astype(bf16), r2, jc, e1, e2)

        h = _ple(h, p[l].reshape(seq, -1), _row(ple_norm[l]), ple_wg[l].astype(bf16), ple_wp[l].astype(bf16),
                 _row(final_norm), final=(l == depth - 1))
    return h.reshape(bsz, seq, D_MODEL)
```

```python
import functools
import math

import jax
import jax.numpy as jnp
import numpy as np
from jax import lax
from jax.experimental import pallas as pl
from jax.experimental.pallas import tpu as pltpu

f32 = jnp.float32
bf16 = jnp.bfloat16
_HI = lax.Precision.HIGHEST
_NT = (((1,), (1,)), ((), ()))
_TN = (((0,), (0,)), ((), ()))

EPS = 1e-6
D_MODEL = 1024
GROUP_W = 256
CONV_K = 4
RET_HEADS = 4
RET_HD = 64
RET_CHUNK = 128
ROPE_BASE = 10000.0
LRU_C = 8.0
S5_GROUPS = 16
S5_GW = 16
S5_STATE = 64
S5_N = S5_GROUPS * S5_STATE
GDN_HEADS = 4
GDN_HD = 64
GDN_CHUNK = 64
PEER_HEADS = 8
PEER_NKEYS = 128
PEER_HALF = 64
PEER_TOPK = 16
PEER_NCAND = 64

IN_PAD = 2944
VMEM_LIMIT = 48 * 1024 * 1024

TB_IN = 256
TB_RET = 512
TB_LRU = 512
TB_S5 = 512
TB_GDN = 256
TB_OUT = 512
TB_TOPK = 256
TB_PEER = 512
NE_TILE = 512
TB_PLE = 512


def _dot(a, b, dims=(((1,), (0,)), ((), ())), hi=False):
    if hi:
        return lax.dot_general(a, b, dims, precision=_HI, preferred_element_type=f32)
    return lax.dot_general(a.astype(bf16), b.astype(bf16), dims, preferred_element_type=f32)


def _split(a):
    hi = a.astype(bf16)
    return hi, (a - hi.astype(f32)).astype(bf16)


def _bdot(a, b, dims=(((1,), (0,)), ((), ()))):
    return lax.dot_general(a, b, dims, preferred_element_type=f32)


def _dot3(a, b, dims=(((1,), (0,)), ((), ()))):
    (ah, al), (bh, bl) = a, b
    return _bdot(ah, bh, dims) + (_bdot(ah, bl, dims) + _bdot(al, bh, dims))


def _dot_sel(x, sel):
    xh, xl = _split(x)
    s = sel.astype(bf16)
    return _bdot(xh, s) + _bdot(xl, s)


def _dot_sel_lhs(sel, x):
    xh, xl = _split(x)
    return _bdot(sel, xh) + _bdot(sel, xl)


def _rms(x, w):
    return x * lax.rsqrt(jnp.mean(x * x, axis=-1, keepdims=True) + EPS) * w


def _shift_rows(x, d, fill):
    row = lax.broadcasted_iota(jnp.int32, x.shape, 0)
    return jnp.where(row >= d, pltpu.roll(x, d, 0), fill)


def _params(n_axes=1):
    return pltpu.CompilerParams(dimension_semantics=("arbitrary",) * n_axes, vmem_limit_bytes=VMEM_LIMIT)


def _full(shape):
    n = len(shape)
    return pl.BlockSpec(shape, lambda *_: (0,) * n)


def _rows(tb, w):
    return pl.BlockSpec((tb, w), lambda i: (i, 0))


def _rope_kernel(pos_ref, freq_ref, cos_ref, sin_ref):
    ang = pos_ref[...] * freq_ref[...]
    cos_ref[...] = jnp.cos(ang)
    sin_ref[...] = jnp.sin(ang)


def _rope_tables(pos, freq):
    s = pos.shape[0]
    tb = min(s, 1024)
    return pl.pallas_call(
        _rope_kernel, grid=(s // tb,),
        in_specs=[_rows(tb, 1), _full((1, 128))],
        out_specs=[_rows(tb, 128), _rows(tb, 128)],
        out_shape=[jax.ShapeDtypeStruct((s, 128), f32)] * 2,
        compiler_params=_params(), name="rope_tables")(pos, freq)


def _in_kernel(h_ref, nw_ref, w_ref, ret_ref, lru_ref, s5_ref, gdn_ref, ba_ref):
    z = _rms(h_ref[...], nw_ref[...]).astype(bf16)
    off = 0
    for ref in (ret_ref, lru_ref, s5_ref, gdn_ref, ba_ref):
        w = ref.shape[1]
        ref[...] = jnp.dot(z, w_ref[:, off:off + w], preferred_element_type=f32)
        off += w


def _in_proj(h, nw, w):
    s = h.shape[0]
    tb = min(s, TB_IN)
    widths = (1024, 512, 256, 1024, 128)
    return pl.pallas_call(
        _in_kernel, grid=(s // tb,),
        in_specs=[_rows(tb, D_MODEL), _full((1, D_MODEL)), _full((D_MODEL, IN_PAD))],
        out_specs=[_rows(tb, w_) for w_ in widths],
        out_shape=[jax.ShapeDtypeStruct((s, w_), f32) for w_ in widths],
        compiler_params=_params(), name="in_proj")(h, nw, w)


def _ret_kernel(x_ref, cos_ref, sin_ref, dmask_ref, qdec_ref, kdec_ref, cdec_ref, bmask_ref, bones_ref,
                gn_ref, bn_ref, out_ref, state_ref, o_scr):
    tb = x_ref.shape[0]

    @pl.when(pl.program_id(0) == 0)
    def _():
        state_ref[...] = jnp.zeros_like(state_ref)

    lane = lax.broadcasted_iota(jnp.int32, (1, GROUP_W), 1)
    first = (lane % RET_HD) < (RET_HD // 2)
    cosf = jnp.concatenate([cos_ref[...], cos_ref[...]], axis=1)
    sinf = jnp.concatenate([sin_ref[...], sin_ref[...]], axis=1) * jnp.where(first, -1.0, 1.0)

    def rot(t):
        partner = jnp.where(first, pltpu.roll(t, GROUP_W - RET_HD // 2, 1), pltpu.roll(t, RET_HD // 2, 1))
        return t * cosf + partner * sinf

    q = rot(x_ref[:, 0:256]) * (RET_HD ** -0.5)
    k = rot(x_ref[:, 256:512])
    v = x_ref[:, 512:768]
    qdec = qdec_ref[...]
    kdec = kdec_ref[...]
    for c in range(tb // RET_CHUNK):
        r0 = c * RET_CHUNK
        qc, kc, vc = q[r0:r0 + RET_CHUNK], k[r0:r0 + RET_CHUNK], v[r0:r0 + RET_CHUNK]
        st = state_ref[...]
        o = _dot(qc, st) * qdec
        for h in range(RET_HEADS):
            hm = (lane // RET_HD) == h
            sc = _dot(jnp.where(hm, qc, 0.0), kc, _NT) * dmask_ref[h]
            o = o + _dot(sc, jnp.where(hm, vc, 0.0))
        kv = _dot(kc * kdec, vc, _TN)
        state_ref[...] = st * cdec_ref[...] + kv * bmask_ref[...]
        o_scr[r0:r0 + RET_CHUNK, :] = o
    o = o_scr[...]
    bones = bones_ref[...]
    mu = _dot(o, bones, hi=True) * (1.0 / RET_HD)
    d = o - mu
    var = _dot(d * d, bones, hi=True) * (1.0 / RET_HD)
    on = d * lax.rsqrt(var + EPS) * gn_ref[...]
    g = x_ref[:, 768:1024]
    y = g * jax.nn.sigmoid(g) * on
    out_ref[...] = _rms(y, bn_ref[...])


def _retention(ret_in, cos, sin, consts, gn, bn):
    s = ret_in.shape[0]
    tb = min(s, TB_RET)
    dmask, qdec, kdec, cdec, bmask, bones = consts
    return pl.pallas_call(
        _ret_kernel, grid=(s // tb,),
        in_specs=[_rows(tb, 1024), _rows(tb, 128), _rows(tb, 128), _full(dmask.shape), _full(qdec.shape),
                  _full(kdec.shape), _full(cdec.shape), _full(bmask.shape), _full(bones.shape),
                  _full((1, GROUP_W)), _full((1, GROUP_W))],
        out_specs=_rows(tb, GROUP_W),
        out_shape=jax.ShapeDtypeStruct((s, GROUP_W), f32),
        scratch_shapes=[pltpu.VMEM((GROUP_W, GROUP_W), f32), pltpu.VMEM((tb, GROUP_W), f32)],
        compiler_params=_params(), name="retention")(ret_in, cos, sin, dmask, qdec, kdec, cdec, bmask, bones, gn, bn)


def _lru_kernel(x_ref, cw_ref, cb_ref, wa_ref, ba_ref, wx_ref, bx_ref, lam_ref, bn_ref, out_ref, xbuf, carry_ref):
    tb = x_ref.shape[0]

    @pl.when(pl.program_id(0) == 0)
    def _():
        xbuf[0:8, :] = jnp.zeros((8, GROUP_W), f32)
        carry_ref[...] = jnp.zeros_like(carry_ref)

    x = x_ref[:, 256:512]
    xbuf[8:8 + tb, :] = x
    cw = cw_ref[...]
    xb = (cw[3:4] * x + cw[2:3] * xbuf[7:7 + tb, :] + cw[1:2] * xbuf[6:6 + tb, :]
          + cw[0:1] * xbuf[5:5 + tb, :]) + cb_ref[...]
    xbuf[0:8, :] = xbuf[tb:tb + 8, :]
    r = jax.nn.sigmoid(_dot(xb, wa_ref[...], hi=True) + ba_ref[...])
    i = jax.nn.sigmoid(_dot(xb, wx_ref[...], hi=True) + bx_ref[...])
    log_a = -LRU_C * r * jax.nn.softplus(-lam_ref[...])
    a = jnp.exp(log_a)
    y2 = 2.0 * log_a
    one_minus_a2 = -jnp.tanh(0.5 * y2) * (jnp.exp(y2) + 1.0)
    b = jnp.sqrt(one_minus_a2) * (i * xb)
    d = 1
    while d < tb:
        b = a * _shift_rows(b, d, 0.0) + b
        a = a * _shift_rows(a, d, 1.0)
        d *= 2
    h = b + a * carry_ref[...]
    carry_ref[...] = h[tb - 1:tb, :]
    y = jax.nn.gelu(x_ref[:, 0:256]) * h
    out_ref[...] = _rms(y, bn_ref[...])


def _rglru(lru_in, cw, cb, wa, ba, wx, bx, lam, bn):
    s = lru_in.shape[0]
    tb = min(s, TB_LRU)
    row = _full((1, GROUP_W))
    sq = _full((GROUP_W, GROUP_W))
    return pl.pallas_call(
        _lru_kernel, grid=(s // tb,),
        in_specs=[_rows(tb, 512), _full((CONV_K, GROUP_W)), row, sq, row, sq, row, row, row],
        out_specs=_rows(tb, GROUP_W),
        out_shape=jax.ShapeDtypeStruct((s, GROUP_W), f32),
        scratch_shapes=[pltpu.VMEM((tb + 8, GROUP_W), f32), pltpu.VMEM((1, GROUP_W), f32)],
        compiler_params=_params(), name="rglru")(lru_in, cw, cb, wa, ba, wx, bx, lam, bn)


def _s5_kernel(u_ref, a_ref, bre_ref, bim_ref, cre_ref, cim_ref, d_ref, gw_ref, gb_ref, bn_ref, out_ref, carry_ref):
    tb = u_ref.shape[0]

    @pl.when(pl.program_id(0) == 0)
    def _():
        carry_ref[...] = jnp.zeros_like(carry_ref)

    u = u_ref[...]
    ar = a_ref[0:1, :]
    ai = a_ref[1:2, :]
    xr = _dot(u, bre_ref[...])
    xi = _dot(u, bim_ref[...])
    cr = carry_ref[0:1, :]
    ci = carry_ref[1:2, :]
    row = lax.broadcasted_iota(jnp.int32, xr.shape, 0)
    xr = jnp.where(row == 0, xr + (ar * cr - ai * ci), xr)
    xi = jnp.where(row == 0, xi + (ar * ci + ai * cr), xi)
    d = 1
    while d < tb:
        sr = _shift_rows(xr, d, 0.0)
        si = _shift_rows(xi, d, 0.0)
        xr = xr + (ar * sr - ai * si)
        xi = xi + (ar * si + ai * sr)
        ar, ai = ar * ar - ai * ai, 2.0 * ar * ai
        d *= 2
    carry_ref[0:1, :] = xr[tb - 1:tb, :]
    carry_ref[1:2, :] = xi[tb - 1:tb, :]
    y = _dot(xr, cre_ref[...]) - _dot(xi, cim_ref[...]) + d_ref[...] * u
    zz = _dot(jax.nn.gelu(y), gw_ref[...]) + gb_ref[...]
    o = zz[:, 0:GROUP_W] * jax.nn.sigmoid(zz[:, GROUP_W:2 * GROUP_W])
    out_ref[...] = _rms(o, bn_ref[...])


def _s5(s5_in, a, bre, bim, cre, cim, dd, gw, gb, bn):
    s = s5_in.shape[0]
    tb = min(s, TB_S5)
    row = _full((1, GROUP_W))
    return pl.pallas_call(
        _s5_kernel, grid=(s // tb,),
        in_specs=[_rows(tb, GROUP_W), _full((2, S5_N)), _full((GROUP_W, S5_N)), _full((GROUP_W, S5_N)),
                  _full((S5_N, GROUP_W)), _full((S5_N, GROUP_W)), row, _full((GROUP_W, 2 * GROUP_W)),
                  _full((1, 2 * GROUP_W)), row],
        out_specs=_rows(tb, GROUP_W),
        out_shape=jax.ShapeDtypeStruct((s, GROUP_W), f32),
        scratch_shapes=[pltpu.VMEM((2, S5_N), f32)],
        compiler_params=_params(), name="s5")(s5_in, a, bre, bim, cre, cim, dd, gw, gb, bn)


def _gdn_kernel(x_ref, ba_ref, cw_ref, alog_ref, dtb_ref, eb_ref, ea_ref, bones_ref, nw_ref, bn_ref, out_ref,
                xbuf, state_ref, o_scr):
    tb = x_ref.shape[0]
    c_ = GDN_CHUNK

    @pl.when(pl.program_id(0) == 0)
    def _():
        xbuf[0:8, :] = jnp.zeros((8, 3 * GROUP_W), f32)
        state_ref[...] = jnp.zeros_like(state_ref)

    x = x_ref[:, 0:768]
    xbuf[8:8 + tb, :] = x
    cw = cw_ref[...]
    y = cw[3:4] * x + cw[2:3] * xbuf[7:7 + tb, :] + cw[1:2] * xbuf[6:6 + tb, :] + cw[0:1] * xbuf[5:5 + tb, :]
    xbuf[0:8, :] = xbuf[tb:tb + 8, :]
    y = y * jax.nn.sigmoid(y)
    bones = bones_ref[...]
    q = y[:, 0:256]
    k = y[:, 256:512]
    v = y[:, 512:768]
    q = q * lax.rsqrt(_dot_sel(q * q, bones) + EPS) * (GDN_HD ** -0.5)
    k = k * lax.rsqrt(_dot_sel(k * k, bones) + EPS)
    ba = ba_ref[...]
    beta_b = _dot_sel(jax.nn.sigmoid(ba), eb_ref[...])
    la = -jnp.exp(alog_ref[...]) * jax.nn.softplus(ba + dtb_ref[...])
    la_b = _dot_sel(la, ea_ref[...])

    ri = lax.broadcasted_iota(jnp.int32, (c_, c_), 0)
    ci = lax.broadcasted_iota(jnp.int32, (c_, c_), 1)
    incl = ri >= ci
    strict = ri > ci
    eye = ri == ci
    ltri = incl.astype(bf16)
    ri4 = lax.broadcasted_iota(jnp.int32, (c_, GROUP_W), 0)
    ci4 = lax.broadcasted_iota(jnp.int32, (c_, GROUP_W), 1) % GDN_HD
    lev_masks = []
    b = 1
    while b < c_:
        lev_masks.append(((ri // (2 * b)) == (ci // (2 * b))) & (((ri // b) % 2) == 1) & (((ci // b) % 2) == 0))
        b *= 2

    n_chunks = tb // c_
    items = [(c, h) for c in range(n_chunks) for h in range(GDN_HEADS)]
    per_chunk = []
    for c in range(n_chunks):
        r0 = c * c_
        la_c = la_b[r0:r0 + c_, :]
        gcum = _dot_sel_lhs(ltri, la_c)
        dmat = _dot_sel_lhs(ltri, jnp.where(ri4 > ci4, la_c, 0.0))
        lm = jnp.where(ri4 >= ci4, jnp.exp(dmat), 0.0)
        eg = jnp.exp(gcum)
        gl = gcum[c_ - 1:c_, :]
        qc, kc, vc, bc = q[r0:r0 + c_, :], k[r0:r0 + c_, :], v[r0:r0 + c_, :], beta_b[r0:r0 + c_, :]
        kb = kc * bc
        per_chunk.append(dict(lm=lm, k=kc, q=qc, kb=kb, vb=vc * bc, kbe=kb * eg, qd=qc * eg,
                              kd=kc * jnp.exp(gl - gcum), cd=jnp.exp(gl)))
    lmat_f, lmat, attn = {}, {}, {}
    for (c, h) in items:
        cs = slice(h * GDN_HD, (h + 1) * GDN_HD)
        pc = per_chunk[c]
        lm = pc['lm'][:, cs]
        kh = pc['k'][:, cs]
        lmat_f[c, h] = jnp.where(strict, _dot(pc['kb'][:, cs], kh, _NT) * lm, 0.0)
        lmat[c, h] = _split(lmat_f[c, h])
        attn[c, h] = jnp.where(incl, _dot(pc['q'][:, cs], kh, _NT) * lm, 0.0)
    tinv = {n: _split(jnp.where(eye, 1.0, 0.0) - jnp.where(lev_masks[0], lmat_f[n], 0.0)) for n in items}
    for m in lev_masks[1:]:
        xs = {n: _split(_dot3((jnp.where(m, lmat[n][0], jnp.zeros_like(lmat[n][0])),
                               jnp.where(m, lmat[n][1], jnp.zeros_like(lmat[n][1]))), tinv[n])) for n in items}
        tinv = {n: _split(tinv[n][0].astype(f32) + tinv[n][1].astype(f32) - _dot3(tinv[n], xs[n])) for n in items}
    u, w, mm, nn, pp, rr = {}, {}, {}, {}, {}, {}
    for (c, h) in items:
        cs = slice(h * GDN_HD, (h + 1) * GDN_HD)
        pc = per_chunk[c]
        u[c, h] = _dot3(tinv[c, h], _split(pc['vb'][:, cs]))
        w[c, h] = _dot3(tinv[c, h], _split(pc['kbe'][:, cs]))
    for (c, h) in items:
        cs = slice(h * GDN_HD, (h + 1) * GDN_HD)
        pc = per_chunk[c]
        kd = _split(pc['kd'][:, cs])
        mm[c, h] = _split(jnp.where(eye, pc['cd'][:, cs], 0.0) - _dot3(kd, _split(w[c, h]), _TN))
        nn[c, h] = _dot3(kd, _split(u[c, h]), _TN)
        pp[c, h] = _dot(attn[c, h], u[c, h])
        rr[c, h] = _split(pc['qd'][:, cs] - _dot(attn[c, h], w[c, h]))
    for c in range(n_chunks):
        r0 = c * c_
        for h in range(GDN_HEADS):
            cs = slice(h * GDN_HD, (h + 1) * GDN_HD)
            st = _split(state_ref[h])
            o_scr[r0:r0 + c_, cs] = pp[c, h] + _dot3(rr[c, h], st)
            state_ref[h] = _dot3(mm[c, h], st) + nn[c, h]
    o = o_scr[...]
    on = o * lax.rsqrt(_dot_sel(o * o, bones) * (1.0 / GDN_HD) + EPS) * nw_ref[...]
    g = x_ref[:, 768:1024]
    yy = on * (g * jax.nn.sigmoid(g))
    out_ref[...] = _rms(yy, bn_ref[...])


def _gdn(gdn_in, ba, cw, alog, dtb, eb, ea, bones, nw, bn):
    s = gdn_in.shape[0]
    tb = min(s, TB_GDN)
    row = _full((1, GROUP_W))
    return pl.pallas_call(
        _gdn_kernel, grid=(s // tb,),
        in_specs=[_rows(tb, 1024), _rows(tb, 128), _full((CONV_K, 3 * GROUP_W)), _full((1, 128)), _full((1, 128)),
                  _full((128, GROUP_W)), _full((128, GROUP_W)), _full((GROUP_W, GROUP_W)), row, row],
        out_specs=_rows(tb, GROUP_W),
        out_shape=jax.ShapeDtypeStruct((s, GROUP_W), f32),
        scratch_shapes=[pltpu.VMEM((tb + 8, 3 * GROUP_W), f32), pltpu.VMEM((GDN_HEADS, GDN_HD, GDN_HD), f32),
                        pltpu.VMEM((tb, GROUP_W), f32)],
        compiler_params=_params(), name="gated_deltanet")(gdn_in, ba, cw, alog, dtb, eb, ea, bones, nw, bn)


def _out_kernel(h_ref, m0_ref, m1_ref, m2_ref, m3_ref, w_ref, out_ref):
    acc = h_ref[...]
    for j, m in enumerate((m0_ref, m1_ref, m2_ref, m3_ref)):
        acc = acc + jnp.dot(m[...].astype(bf16), w_ref[j * GROUP_W:(j + 1) * GROUP_W, :], preferred_element_type=f32)
    out_ref[...] = acc


def _out_proj(h, mixed, w):
    s = h.shape[0]
    tb = min(s, TB_OUT)
    return pl.pallas_call(
        _out_kernel, grid=(s // tb,),
        in_specs=[_rows(tb, D_MODEL)] + [_rows(tb, GROUP_W)] * 4 + [_full((D_MODEL, D_MODEL))],
        out_specs=_rows(tb, D_MODEL),
        out_shape=jax.ShapeDtypeStruct((s, D_MODEL), f32),
        compiler_params=_params(), name="out_proj")(h, *mixed, w)


def _extract_topk(s, idx_col, n_bad, put):
    for r in range(PEER_TOPK):
        m = jnp.max(s, axis=0, keepdims=True)
        idx = jnp.min(jnp.where(s == m, idx_col, n_bad), axis=0, keepdims=True)
        oh = idx_col == idx
        put(r, m, oh)
        s = jnp.where(oh, -jnp.inf, s)


def _extract_topk_distinct(s, put):
    for r in range(PEER_TOPK):
        m = jnp.max(s, axis=0, keepdims=True)
        oh = s == m
        put(r, m, oh)
        s = jnp.where(oh, -jnp.inf, s)


def _any_count_off(marks):
    n = jnp.sum(marks, axis=0, keepdims=True)
    return jnp.max(jnp.abs(n - float(PEER_TOPK))) > 0.5


def _topk_kernel(h_ref, nw_ref, wqt_ref, sk_ref, p1_ref, p2_ref, cidx_ref, qsel_ref,
                 zt_ref, r2_ref, jc_ref, e1_ref, e2_ref, s_scr, rank_scr, val_scr, sel_scr):
    tb = h_ref.shape[0]
    z = _rms(h_ref[...], nw_ref[...])
    zt = z.T.astype(bf16)
    zt_ref[...] = zt
    qt = jnp.dot(wqt_ref[...], zt, preferred_element_type=f32)
    for hp in range(2 * PEER_HEADS):
        s_scr[hp] = _dot(sk_ref[hp], qt[hp * PEER_HALF:(hp + 1) * PEER_HALF, :])

    key_idx = lax.broadcasted_iota(jnp.int32, (PEER_NKEYS, tb), 0).astype(f32)

    def stage1(hp, carry):
        def run(extract):
            rank = [jnp.full((PEER_NKEYS, tb), float(PEER_TOPK), f32)]

            def put(r, m, oh):
                val_scr[hp, r:r + 1, :] = m
                rank[0] = jnp.where(oh, float(r), rank[0])

            extract(s_scr[hp], put)
            rank_scr[hp] = rank[0]
            return rank[0]

        rank = run(_extract_topk_distinct)

        @pl.when(_any_count_off(jnp.where(rank < float(PEER_TOPK), 1.0, 0.0)))
        def _():
            run(lambda s, put: _extract_topk(s, key_idx, float(PEER_NKEYS), put))

        return carry

    lax.fori_loop(0, 2 * PEER_HEADS, stage1, 0)

    cidx = jnp.broadcast_to(cidx_ref[...], (PEER_NCAND, tb))
    p1 = p1_ref[...]
    p2 = p2_ref[...]

    def stage2(h, carry):
        v1 = val_scr[2 * h]
        v2 = val_scr[2 * h + 1]
        cand = _dot(p1, v1, hi=True) + _dot(p2, v2, hi=True)
        cand = jnp.where(cidx < 256.0, cand, -jnp.inf)
        def run(extract):
            sel = [jnp.zeros((PEER_NCAND, tb), f32)]

            def put(r, m, oh):
                sel[0] = jnp.where(oh, 1.0, sel[0])

            extract(cand, put)
            sel_scr[...] = sel[0]
            return sel[0]

        sel_fast = run(_extract_topk_distinct)

        @pl.when(_any_count_off(sel_fast))
        def _():
            run(lambda s, put: _extract_topk(s, cidx, 1e9, put))

        sel = sel_scr[...]
        m1 = v1[0:1, :]
        m2 = v2[0:1, :]
        wgt = sel * _dot_sel_lhs(p1.astype(bf16), jnp.exp(v1 - m1)) * _dot_sel_lhs(p2.astype(bf16), jnp.exp(v2 - m2))
        zsum = jnp.sum(wgt, axis=0, keepdims=True)
        cnt = _bdot(qsel_ref[...].astype(bf16), sel.astype(bf16))
        r1 = rank_scr[2 * h]
        r2 = rank_scr[2 * h + 1]
        jc = jnp.zeros((PEER_NKEYS, tb), f32)
        for i in range(PEER_TOPK):
            jc = jnp.where(r1 == float(i), cnt[i:i + 1, :], jc)
        r2_ref[h] = r2.astype(bf16)
        jc_ref[h] = jc
        e1_ref[h] = jnp.where(r1 < float(PEER_TOPK), jnp.exp(s_scr[2 * h] - m1), 0.0)
        e2_ref[h] = (jnp.where(r2 < float(PEER_TOPK), jnp.exp(s_scr[2 * h + 1] - m2), 0.0) / zsum).astype(bf16)
        return carry

    lax.fori_loop(0, PEER_HEADS, stage2, 0)


def _peer_topk(h, nw, wqt, sk, p1, p2, cidx, qsel):
    s = h.shape[0]
    tb = min(s, TB_TOPK)
    gate_spec = pl.BlockSpec((PEER_HEADS, PEER_NKEYS, tb), lambda i: (0, 0, i))
    gate_shape = jax.ShapeDtypeStruct((PEER_HEADS, PEER_NKEYS, s), f32)
    gate_shape16 = jax.ShapeDtypeStruct((PEER_HEADS, PEER_NKEYS, s), bf16)
    return pl.pallas_call(
        _topk_kernel, grid=(s // tb,),
        in_specs=[_rows(tb, D_MODEL), _full((1, D_MODEL)), _full((D_MODEL, D_MODEL)),
                  _full((2 * PEER_HEADS, PEER_NKEYS, PEER_HALF)), _full((PEER_NCAND, PEER_TOPK)),
                  _full((PEER_NCAND, PEER_TOPK)), _full((PEER_NCAND, 1)), _full((PEER_TOPK, PEER_NCAND))],
        out_specs=[pl.BlockSpec((D_MODEL, tb), lambda i: (0, i)), gate_spec, gate_spec, gate_spec, gate_spec],
        out_shape=[jax.ShapeDtypeStruct((D_MODEL, s), bf16), gate_shape16, gate_shape, gate_shape, gate_shape16],
        scratch_shapes=[pltpu.VMEM((2 * PEER_HEADS, PEER_NKEYS, tb), f32),
                        pltpu.VMEM((2 * PEER_HEADS, PEER_NKEYS, tb), f32),
                        pltpu.VMEM((2 * PEER_HEADS, PEER_TOPK, tb), f32), pltpu.VMEM((PEER_NCAND, tb), f32)],
        compiler_params=_params(), name="peer_topk")(h, nw, wqt, sk, p1, p2, cidx, qsel)


def _peer_kernel(zt_ref, h_ref, u_ref, vt_ref, r2_ref, jc_ref, e1_ref, e2_ref, out_ref, acc_ref, act_ref, hh_ref):
    j = pl.program_id(1)
    tb = h_ref.shape[0]
    ne = u_ref.shape[0]
    na = ne // PEER_NKEYS
    n_tiles = pl.num_programs(1) - 2

    @pl.when(j == 0)
    def _():
        acc_ref[...] = jnp.zeros_like(acc_ref)
        act_ref[...] = jnp.zeros_like(act_ref)
        hh_ref[...] = jnp.zeros_like(hh_ref)

    cur = j % 2
    prev = 1 - cur
    tile = jnp.clip(j - 1, 0, n_tiles - 1)
    rows_v = acc_ref.shape[0] // na
    for ai in range(na):
        v0 = ai * rows_v
        acc_ref[v0:v0 + rows_v, :] += jnp.dot(vt_ref[v0:v0 + rows_v, :], hh_ref[cur],
                                               preferred_element_type=f32)
        a = tile * na + ai
        g = jnp.zeros((PEER_NKEYS, tb), bf16)
        for h in range(PEER_HEADS):
            jc = jc_ref[h, pl.ds(a, 1), :].astype(bf16)
            e1 = e1_ref[h, pl.ds(a, 1), :].astype(bf16)
            g = g + jnp.where(r2_ref[h] < jc, e2_ref[h] * e1, jnp.zeros_like(g))
        r0 = ai * PEER_NKEYS
        hh_ref[prev, r0:r0 + PEER_NKEYS, :] = g * jax.nn.gelu(act_ref[prev, r0:r0 + PEER_NKEYS, :].astype(bf16))
        act_ref[cur, r0:r0 + PEER_NKEYS, :] = jnp.dot(u_ref[r0:r0 + PEER_NKEYS, :], zt_ref[...],
                                                      preferred_element_type=f32)

    @pl.when(j == pl.num_programs(1) - 1)
    def _():
        out_ref[...] = h_ref[...] + acc_ref[...].T


def _peer_dense(zt, h, u, vt, r2, jc, e1, e2):
    s = h.shape[0]
    tb = min(s, TB_PEER)
    n_tiles = u.shape[0] // NE_TILE
    gate_spec = pl.BlockSpec((PEER_HEADS, PEER_NKEYS, tb), lambda i, j: (0, 0, i))
    return pl.pallas_call(
        _peer_kernel, grid=(s // tb, n_tiles + 2),
        in_specs=[pl.BlockSpec((D_MODEL, tb), lambda i, j: (0, i)), pl.BlockSpec((tb, D_MODEL), lambda i, j: (i, 0)),
                  pl.BlockSpec((NE_TILE, D_MODEL), lambda i, j: (jnp.minimum(j, n_tiles - 1), 0)),
                  pl.BlockSpec((None, D_MODEL, NE_TILE), lambda i, j: (jnp.clip(j - 2, 0, n_tiles - 1), 0, 0)),
                  gate_spec, gate_spec, gate_spec, gate_spec],
        out_specs=pl.BlockSpec((tb, D_MODEL), lambda i, j: (i, 0)),
        out_shape=jax.ShapeDtypeStruct((s, D_MODEL), f32),
        scratch_shapes=[pltpu.VMEM((D_MODEL, tb), f32), pltpu.VMEM((2, NE_TILE, tb), f32),
                        pltpu.VMEM((2, NE_TILE, tb), bf16)],
        compiler_params=_params(2), name="peer_dense")(zt, h, u, vt, r2, jc, e1, e2)


def _ple_kernel(h_ref, p_ref, nw_ref, wg_ref, wp_ref, fw_ref, out_ref, *, final):
    x = h_ref[...]
    gate = jax.nn.sigmoid(jnp.dot(_rms(x, nw_ref[...]).astype(bf16), wg_ref[...], preferred_element_type=f32))
    y = x + jnp.dot(p_ref[...].astype(bf16), wp_ref[...], preferred_element_type=f32) * gate
    if final:
        y = _rms(y, fw_ref[...])
    out_ref[...] = y


def _ple(h, p, nw, wg, wp, fw, final):
    s = h.shape[0]
    tb = min(s, TB_PLE)
    pw = p.shape[1]
    return pl.pallas_call(
        functools.partial(_ple_kernel, final=final), grid=(s // tb,),
        in_specs=[_rows(tb, D_MODEL), _rows(tb, pw), _full((1, D_MODEL)), _full((D_MODEL, D_MODEL)),
                  _full((pw, D_MODEL)), _full((1, D_MODEL))],
        out_specs=_rows(tb, D_MODEL),
        out_shape=jax.ShapeDtypeStruct((s, D_MODEL), f32),
        compiler_params=_params(), name="ple_gate")(h, p, nw, wg, wp, fw)


def _block_diag(blocks):
    n, r, c = blocks.shape
    return jnp.einsum('nrc,nm->nrmc', blocks, jnp.eye(n, dtype=blocks.dtype)).reshape(n * r, n * c)


def _retention_consts():
    log_gamma = jnp.log(1.0 - 2.0 ** (-5.0 - jnp.arange(RET_HEADS, dtype=f32)))
    idx = jnp.arange(RET_CHUNK, dtype=f32)
    diff = idx[:, None] - idx[None, :]
    causal = diff >= 0
    dmask = jnp.where(causal, jnp.exp(log_gamma[:, None, None] * jnp.where(causal, diff, 0.0)), 0.0)
    q_decay = jnp.exp(log_gamma[:, None] * (idx + 1.0))
    k_decay = jnp.exp(log_gamma[:, None] * (RET_CHUNK - 1.0 - idx))
    chunk_decay = jnp.exp(log_gamma * RET_CHUNK)
    qdec = jnp.repeat(q_decay.T, RET_HD, axis=1)
    kdec = jnp.repeat(k_decay.T, RET_HD, axis=1)
    bmask = _block_diag(jnp.ones((RET_HEADS, RET_HD, RET_HD), f32))
    cdec = _block_diag(jnp.broadcast_to(chunk_decay[:, None, None], (RET_HEADS, RET_HD, RET_HD)))
    return dmask, qdec, kdec, cdec, bmask, bmask


def _s5_discretize(a_re, a_im, b_re, b_im, c_re, c_im, log_dt):
    dt = jnp.exp(log_dt)[:, None]
    mag = jnp.exp(a_re * dt)
    ang = a_im * dt
    ab_re, ab_im = mag * jnp.cos(ang), mag * jnp.sin(ang)
    den = a_re * a_re + a_im * a_im
    p_re, p_im = ab_re - 1.0, ab_im
    f_re = (p_re * a_re + p_im * a_im) / den
    f_im = (p_im * a_re - p_re * a_im) / den
    bb_re = f_re[..., None] * b_re - f_im[..., None] * b_im
    bb_im = f_re[..., None] * b_im + f_im[..., None] * b_re
    abar = jnp.stack([ab_re.reshape(-1), ab_im.reshape(-1)])
    bre = _block_diag(jnp.swapaxes(bb_re, 1, 2))
    bim = _block_diag(jnp.swapaxes(bb_im, 1, 2))
    cre = _block_diag(jnp.swapaxes(c_re, 1, 2))
    cim = _block_diag(jnp.swapaxes(c_im, 1, 2))
    return abar, bre, bim, cre, cim


def _peer_rank_consts():
    pairs = [(i, j) for i in range(PEER_TOPK) for j in range(PEER_TOPK) if (i + 1) * (j + 1) <= PEER_TOPK]
    p1 = np.zeros((PEER_NCAND, PEER_TOPK), np.float32)
    p2 = np.zeros((PEER_NCAND, PEER_TOPK), np.float32)
    cidx = np.full((PEER_NCAND, 1), 1e9, np.float32)
    for c, (i, j) in enumerate(pairs):
        p1[c, i] = 1.0
        p2[c, j] = 1.0
        cidx[c, 0] = PEER_TOPK * i + j
    return jnp.asarray(p1), jnp.asarray(p2), jnp.asarray(cidx), jnp.asarray(p1.T)


def _row(v):
    return v.reshape(1, -1).astype(f32)


def kernel(x, p, positions, mix_norm, w_in, ret_gn, lru_conv_w, lru_conv_b, lru_w_a, lru_b_a, lru_w_x, lru_b_x, lru_lambda, s5_a_re, s5_a_im, s5_b_re, s5_b_im, s5_c_re, s5_c_im, s5_d, s5_log_dt, s5_glu_w, s5_glu_b, gdn_conv_w, gdn_a_log, gdn_dt_bias, gdn_norm, branch_norm, w_out, ffn_norm, peer_wq, peer_subkeys, peer_u, peer_v, ple_norm, ple_wg, ple_wp, final_norm):
    bsz, seq, _ = x.shape
    depth = w_in.shape[0]
    assert bsz == 1
    h = x.reshape(seq, D_MODEL)

    half = RET_HD // 2
    inv_freq = ROPE_BASE ** (-jnp.arange(half, dtype=f32) / half)
    freq = jnp.tile(inv_freq, 128 // half).reshape(1, 128)
    cos, sin = _rope_tables(positions.reshape(seq, 1).astype(f32), freq)

    ret_consts = _retention_consts()
    bones = ret_consts[5]
    p1, p2, cidx, qsel = _peer_rank_consts()
    head_rows = jnp.arange(128)[:, None]
    head_lanes = jnp.arange(GROUP_W)[None, :] // GDN_HD
    eb = (head_rows == head_lanes).astype(f32)
    ea = (head_rows == head_lanes + GDN_HEADS).astype(f32)

    for l in range(depth):
        w = w_in[l]
        w_perm = jnp.concatenate([w[:, 0:2560], w[:, 2568:2824], w[:, 2560:2568],
                                  jnp.zeros((D_MODEL, IN_PAD - 2824), f32)], axis=1).astype(bf16)
        ret_in, lru_in, s5_in, gdn_in, ba_in = _in_proj(h, _row(mix_norm[l]), w_perm)

        y_ret = _retention(ret_in, cos, sin, ret_consts, _row(ret_gn[l]), _row(branch_norm[l, 0]))
        y_lru = _rglru(lru_in, lru_conv_w[l], _row(lru_conv_b[l]), _block_diag(lru_w_a[l]), _row(lru_b_a[l]),
                       _block_diag(lru_w_x[l]), _row(lru_b_x[l]), _row(lru_lambda[l]), _row(branch_norm[l, 1]))
        abar, bre, bim, cre, cim = _s5_discretize(s5_a_re[l], s5_a_im[l], s5_b_re[l], s5_b_im[l],
                                                  s5_c_re[l], s5_c_im[l], s5_log_dt[l])
        y_s5 = _s5(s5_in, abar, bre, bim, cre, cim, _row(s5_d[l]), s5_glu_w[l], _row(s5_glu_b[l]),
                   _row(branch_norm[l, 2]))
        alog = jnp.zeros((1, 128), f32).at[0, GDN_HEADS:2 * GDN_HEADS].set(gdn_a_log[l])
        dtb = jnp.zeros((1, 128), f32).at[0, GDN_HEADS:2 * GDN_HEADS].set(gdn_dt_bias[l])
        y_gdn = _gdn(gdn_in, ba_in, gdn_conv_w[l], alog, dtb, eb, ea, bones,
                     _row(jnp.tile(gdn_norm[l], GDN_HEADS)), _row(branch_norm[l, 3]))

        h = _out_proj(h, (y_ret, y_lru, y_s5, y_gdn), w_out[l].astype(bf16))

        sk = peer_subkeys[l].reshape(2 * PEER_HEADS, PEER_NKEYS, PEER_HALF)
        zt, r2, jc, e1, e2 = _peer_topk(h, _row(ffn_norm[l]), peer_wq[l].T.astype(bf16), sk, p1, p2, cidx, qsel)
        vt_tiles = peer_v[l].astype(bf16).reshape(-1, NE_TILE, D_MODEL).swapaxes(1, 2)
        h = _peer_dense(zt, h, peer_u[l].astype(bf16), vt_tiles, r2, jc, e1, e2)

        h = _ple(h, p[l].reshape(seq, -1), _row(ple_norm[l]), ple_wg[l].astype(bf16), ple_wp[l].astype(bf16),
                 _row(final_norm), final=(l == depth - 1))
    return h.reshape(bsz, seq, D_MODEL)
```

```python
import functools
import math

import jax
import jax.numpy as jnp
import numpy as np
from jax import lax
from jax.experimental import pallas as pl
from jax.experimental.pallas import tpu as pltpu

f32 = jnp.float32
bf16 = jnp.bfloat16
_HI = lax.Precision.HIGHEST
_NT = (((1,), (1,)), ((), ()))
_TN = (((0,), (0,)), ((), ()))

EPS = 1e-6
D_MODEL = 1024
GROUP_W = 256
CONV_K = 4
RET_HEADS = 4
RET_HD = 64
RET_CHUNK = 128
ROPE_BASE = 10000.0
LRU_C = 8.0
S5_GROUPS = 16
S5_GW = 16
S5_STATE = 64
S5_N = S5_GROUPS * S5_STATE
GDN_HEADS = 4
GDN_HD = 64
GDN_CHUNK = 64
PEER_HEADS = 8
PEER_NKEYS = 128
PEER_HALF = 64
PEER_TOPK = 16
PEER_NCAND = 64

IN_PAD = 2944
VMEM_LIMIT = 48 * 1024 * 1024

TB_IN = 256
TB_RET = 512
TB_LRU = 512
TB_S5 = 512
TB_GDN = 256
TB_OUT = 512
TB_TOPK = 256
TB_PEER = 512
NE_TILE = 512
TB_PLE = 512


def _dot(a, b, dims=(((1,), (0,)), ((), ())), hi=False):
    if hi:
        return lax.dot_general(a, b, dims, precision=_HI, preferred_element_type=f32)
    return lax.dot_general(a.astype(bf16), b.astype(bf16), dims, preferred_element_type=f32)


def _split(a):
    hi = a.astype(bf16)
    return hi, (a - hi.astype(f32)).astype(bf16)


def _bdot(a, b, dims=(((1,), (0,)), ((), ()))):
    return lax.dot_general(a, b, dims, preferred_element_type=f32)


def _dot3(a, b, dims=(((1,), (0,)), ((), ()))):
    (ah, al), (bh, bl) = a, b
    return _bdot(ah, bh, dims) + (_bdot(ah, bl, dims) + _bdot(al, bh, dims))


def _dot_sel(x, sel):
    xh, xl = _split(x)
    s = sel.astype(bf16)
    return _bdot(xh, s) + _bdot(xl, s)


def _dot_sel_lhs(sel, x):
    xh, xl = _split(x)
    return _bdot(sel, xh) + _bdot(sel, xl)


def _rms(x, w):
    return x * lax.rsqrt(jnp.mean(x * x, axis=-1, keepdims=True) + EPS) * w


def _shift_rows(x, d, fill):
    row = lax.broadcasted_iota(jnp.int32, x.shape, 0)
    return jnp.where(row >= d, pltpu.roll(x, d, 0), fill)


def _params(n_axes=1):
    return pltpu.CompilerParams(dimension_semantics=("arbitrary",) * n_axes, vmem_limit_bytes=VMEM_LIMIT)


def _full(shape):
    n = len(shape)
    return pl.BlockSpec(shape, lambda *_: (0,) * n)


def _rows(tb, w):
    return pl.BlockSpec((tb, w), lambda i: (i, 0))


def _rope_kernel(pos_ref, freq_ref, cos_ref, sin_ref):
    ang = pos_ref[...] * freq_ref[...]
    cos_ref[...] = jnp.cos(ang)
    sin_ref[...] = jnp.sin(ang)


def _rope_tables(pos, freq):
    s = pos.shape[0]
    tb = min(s, 1024)
    return pl.pallas_call(
        _rope_kernel, grid=(s // tb,),
        in_specs=[_rows(tb, 1), _full((1, 128))],
        out_specs=[_rows(tb, 128), _rows(tb, 128)],
        out_shape=[jax.ShapeDtypeStruct((s, 128), f32)] * 2,
        compiler_params=_params(), name="rope_tables")(pos, freq)


def _in_kernel(h_ref, nw_ref, w_ref, ret_ref, lru_ref, s5_ref, gdn_ref, ba_ref):
    z = _rms(h_ref[...], nw_ref[...]).astype(bf16)
    off = 0
    for ref in (ret_ref, lru_ref, s5_ref, gdn_ref, ba_ref):
        w = ref.shape[1]
        ref[...] = jnp.dot(z, w_ref[:, off:off + w], preferred_element_type=f32)
        off += w


def _in_proj(h, nw, w):
    s = h.shape[0]
    tb = min(s, TB_IN)
    widths = (1024, 512, 256, 1024, 128)
    return pl.pallas_call(
        _in_kernel, grid=(s // tb,),
        in_specs=[_rows(tb, D_MODEL), _full((1, D_MODEL)), _full((D_MODEL, IN_PAD))],
        out_specs=[_rows(tb, w_) for w_ in widths],
        out_shape=[jax.ShapeDtypeStruct((s, w_), f32) for w_ in widths],
        compiler_params=_params(), name="in_proj")(h, nw, w)


def _ret_kernel(x_ref, cos_ref, sin_ref, dmask_ref, qdec_ref, kdec_ref, cdec_ref, bmask_ref, bones_ref,
                gn_ref, bn_ref, out_ref, state_ref, o_scr):
    tb = x_ref.shape[0]

    @pl.when(pl.program_id(0) == 0)
    def _():
        state_ref[...] = jnp.zeros_like(state_ref)

    lane = lax.broadcasted_iota(jnp.int32, (1, GROUP_W), 1)
    first = (lane % RET_HD) < (RET_HD // 2)
    cosf = jnp.concatenate([cos_ref[...], cos_ref[...]], axis=1)
    sinf = jnp.concatenate([sin_ref[...], sin_ref[...]], axis=1) * jnp.where(first, -1.0, 1.0)

    def rot(t):
        partner = jnp.where(first, pltpu.roll(t, GROUP_W - RET_HD // 2, 1), pltpu.roll(t, RET_HD // 2, 1))
        return t * cosf + partner * sinf

    q = rot(x_ref[:, 0:256]) * (RET_HD ** -0.5)
    k = rot(x_ref[:, 256:512])
    v = x_ref[:, 512:768]
    qdec = qdec_ref[...]
    kdec = kdec_ref[...]
    for c in range(tb // RET_CHUNK):
        r0 = c * RET_CHUNK
        qc, kc, vc = q[r0:r0 + RET_CHUNK], k[r0:r0 + RET_CHUNK], v[r0:r0 + RET_CHUNK]
        st = state_ref[...]
        o = _dot(qc, st) * qdec
        for h in range(RET_HEADS):
            hm = (lane // RET_HD) == h
            sc = _dot(jnp.where(hm, qc, 0.0), kc, _NT) * dmask_ref[h]
            o = o + _dot(sc, jnp.where(hm, vc, 0.0))
        kv = _dot(kc * kdec, vc, _TN)
        state_ref[...] = st * cdec_ref[...] + kv * bmask_ref[...]
        o_scr[r0:r0 + RET_CHUNK, :] = o
    o = o_scr[...]
    bones = bones_ref[...]
    mu = _dot(o, bones, hi=True) * (1.0 / RET_HD)
    d = o - mu
    var = _dot(d * d, bones, hi=True) * (1.0 / RET_HD)
    on = d * lax.rsqrt(var + EPS) * gn_ref[...]
    g = x_ref[:, 768:1024]
    y = g * jax.nn.sigmoid(g) * on
    out_ref[...] = _rms(y, bn_ref[...])


def _retention(ret_in, cos, sin, consts, gn, bn):
    s = ret_in.shape[0]
    tb = min(s, TB_RET)
    dmask, qdec, kdec, cdec, bmask, bones = consts
    return pl.pallas_call(
        _ret_kernel, grid=(s // tb,),
        in_specs=[_rows(tb, 1024), _rows(tb, 128), _rows(tb, 128), _full(dmask.shape), _full(qdec.shape),
                  _full(kdec.shape), _full(cdec.shape), _full(bmask.shape), _full(bones.shape),
                  _full((1, GROUP_W)), _full((1, GROUP_W))],
        out_specs=_rows(tb, GROUP_W),
        out_shape=jax.ShapeDtypeStruct((s, GROUP_W), f32),
        scratch_shapes=[pltpu.VMEM((GROUP_W, GROUP_W), f32), pltpu.VMEM((tb, GROUP_W), f32)],
        compiler_params=_params(), name="retention")(ret_in, cos, sin, dmask, qdec, kdec, cdec, bmask, bones, gn, bn)


def _lru_kernel(x_ref, cw_ref, cb_ref, wa_ref, ba_ref, wx_ref, bx_ref, lam_ref, bn_ref, out_ref, xbuf, carry_ref):
    tb = x_ref.shape[0]

    @pl.when(pl.program_id(0) == 0)
    def _():
        xbuf[0:8, :] = jnp.zeros((8, GROUP_W), f32)
        carry_ref[...] = jnp.zeros_like(carry_ref)

    x = x_ref[:, 256:512]
    xbuf[8:8 + tb, :] = x
    cw = cw_ref[...]
    xb = (cw[3:4] * x + cw[2:3] * xbuf[7:7 + tb, :] + cw[1:2] * xbuf[6:6 + tb, :]
          + cw[0:1] * xbuf[5:5 + tb, :]) + cb_ref[...]
    xbuf[0:8, :] = xbuf[tb:tb + 8, :]
    r = jax.nn.sigmoid(_dot(xb, wa_ref[...], hi=True) + ba_ref[...])
    i = jax.nn.sigmoid(_dot(xb, wx_ref[...], hi=True) + bx_ref[...])
    log_a = -LRU_C * r * jax.nn.softplus(-lam_ref[...])
    a = jnp.exp(log_a)
    y2 = 2.0 * log_a
    one_minus_a2 = -jnp.tanh(0.5 * y2) * (jnp.exp(y2) + 1.0)
    b = jnp.sqrt(one_minus_a2) * (i * xb)
    d = 1
    while d < tb:
        b = a * _shift_rows(b, d, 0.0) + b
        a = a * _shift_rows(a, d, 1.0)
        d *= 2
    h = b + a * carry_ref[...]
    carry_ref[...] = h[tb - 1:tb, :]
    y = jax.nn.gelu(x_ref[:, 0:256]) * h
    out_ref[...] = _rms(y, bn_ref[...])


def _rglru(lru_in, cw, cb, wa, ba, wx, bx, lam, bn):
    s = lru_in.shape[0]
    tb = min(s, TB_LRU)
    row = _full((1, GROUP_W))
    sq = _full((GROUP_W, GROUP_W))
    return pl.pallas_call(
        _lru_kernel, grid=(s // tb,),
        in_specs=[_rows(tb, 512), _full((CONV_K, GROUP_W)), row, sq, row, sq, row, row, row],
        out_specs=_rows(tb, GROUP_W),
        out_shape=jax.ShapeDtypeStruct((s, GROUP_W), f32),
        scratch_shapes=[pltpu.VMEM((tb + 8, GROUP_W), f32), pltpu.VMEM((1, GROUP_W), f32)],
        compiler_params=_params(), name="rglru")(lru_in, cw, cb, wa, ba, wx, bx, lam, bn)


def _s5_kernel(u_ref, a_ref, bre_ref, bim_ref, cre_ref, cim_ref, d_ref, gw_ref, gb_ref, bn_ref, out_ref, carry_ref):
    tb = u_ref.shape[0]

    @pl.when(pl.program_id(0) == 0)
    def _():
        carry_ref[...] = jnp.zeros_like(carry_ref)

    u = u_ref[...]
    ar = a_ref[0:1, :]
    ai = a_ref[1:2, :]
    xr = _dot(u, bre_ref[...])
    xi = _dot(u, bim_ref[...])
    cr = carry_ref[0:1, :]
    ci = carry_ref[1:2, :]
    row = lax.broadcasted_iota(jnp.int32, xr.shape, 0)
    xr = jnp.where(row == 0, xr + (ar * cr - ai * ci), xr)
    xi = jnp.where(row == 0, xi + (ar * ci + ai * cr), xi)
    d = 1
    while d < tb:
        sr = _shift_rows(xr, d, 0.0)
        si = _shift_rows(xi, d, 0.0)
        xr = xr + (ar * sr - ai * si)
        xi = xi + (ar * si + ai * sr)
        ar, ai = ar * ar - ai * ai, 2.0 * ar * ai
        d *= 2
    carry_ref[0:1, :] = xr[tb - 1:tb, :]
    carry_ref[1:2, :] = xi[tb - 1:tb, :]
    y = _dot(xr, cre_ref[...]) - _dot(xi, cim_ref[...]) + d_ref[...] * u
    zz = _dot(jax.nn.gelu(y), gw_ref[...]) + gb_ref[...]
    o = zz[:, 0:GROUP_W] * jax.nn.sigmoid(zz[:, GROUP_W:2 * GROUP_W])
    out_ref[...] = _rms(o, bn_ref[...])


def _s5(s5_in, a, bre, bim, cre, cim, dd, gw, gb, bn):
    s = s5_in.shape[0]
    tb = min(s, TB_S5)
    row = _full((1, GROUP_W))
    return pl.pallas_call(
        _s5_kernel, grid=(s // tb,),
        in_specs=[_rows(tb, GROUP_W), _full((2, S5_N)), _full((GROUP_W, S5_N)), _full((GROUP_W, S5_N)),
                  _full((S5_N, GROUP_W)), _full((S5_N, GROUP_W)), row, _full((GROUP_W, 2 * GROUP_W)),
                  _full((1, 2 * GROUP_W)), row],
        out_specs=_rows(tb, GROUP_W),
        out_shape=jax.ShapeDtypeStruct((s, GROUP_W), f32),
        scratch_shapes=[pltpu.VMEM((2, S5_N), f32)],
        compiler_params=_params(), name="s5")(s5_in, a, bre, bim, cre, cim, dd, gw, gb, bn)


def _gdn_kernel(x_ref, ba_ref, cw_ref, alog_ref, dtb_ref, eb_ref, ea_ref, bones_ref, nw_ref, bn_ref, out_ref,
                xbuf, state_ref, o_scr):
    tb = x_ref.shape[0]
    c_ = GDN_CHUNK

    @pl.when(pl.program_id(0) == 0)
    def _():
        xbuf[0:8, :] = jnp.zeros((8, 3 * GROUP_W), f32)
        state_ref[...] = jnp.zeros_like(state_ref)

    x = x_ref[:, 0:768]
    xbuf[8:8 + tb, :] = x
    cw = cw_ref[...]
    y = cw[3:4] * x + cw[2:3] * xbuf[7:7 + tb, :] + cw[1:2] * xbuf[6:6 + tb, :] + cw[0:1] * xbuf[5:5 + tb, :]
    xbuf[0:8, :] = xbuf[tb:tb + 8, :]
    y = y * jax.nn.sigmoid(y)
    bones = bones_ref[...]
    q = y[:, 0:256]
    k = y[:, 256:512]
    v = y[:, 512:768]
    q = q * lax.rsqrt(_dot_sel(q * q, bones) + EPS) * (GDN_HD ** -0.5)
    k = k * lax.rsqrt(_dot_sel(k * k, bones) + EPS)
    ba = ba_ref[...]
    beta_b = _dot_sel(jax.nn.sigmoid(ba), eb_ref[...])
    la = -jnp.exp(alog_ref[...]) * jax.nn.softplus(ba + dtb_ref[...])
    la_b = _dot_sel(la, ea_ref[...])

    ri = lax.broadcasted_iota(jnp.int32, (c_, c_), 0)
    ci = lax.broadcasted_iota(jnp.int32, (c_, c_), 1)
    incl = ri >= ci
    strict = ri > ci
    eye = ri == ci
    ltri = incl.astype(bf16)
    ri4 = lax.broadcasted_iota(jnp.int32, (c_, GROUP_W), 0)
    ci4 = lax.broadcasted_iota(jnp.int32, (c_, GROUP_W), 1) % GDN_HD
    lev_masks = []
    b = 1
    while b < c_:
        lev_masks.append(((ri // (2 * b)) == (ci // (2 * b))) & (((ri // b) % 2) == 1) & (((ci // b) % 2) == 0))
        b *= 2

    n_chunks = tb // c_
    items = [(c, h) for c in range(n_chunks) for h in range(GDN_HEADS)]
    per_chunk = []
    for c in range(n_chunks):
        r0 = c * c_
        la_c = la_b[r0:r0 + c_, :]
        gcum = _dot_sel_lhs(ltri, la_c)
        dmat = _dot_sel_lhs(ltri, jnp.where(ri4 > ci4, la_c, 0.0))
        lm = jnp.where(ri4 >= ci4, jnp.exp(dmat), 0.0)
        eg = jnp.exp(gcum)
        gl = gcum[c_ - 1:c_, :]
        qc, kc, vc, bc = q[r0:r0 + c_, :], k[r0:r0 + c_, :], v[r0:r0 + c_, :], beta_b[r0:r0 + c_, :]
        kb = kc * bc
        per_chunk.append(dict(lm=lm, k=kc, q=qc, kb=kb, vb=vc * bc, kbe=kb * eg, qd=qc * eg,
                              kd=kc * jnp.exp(gl - gcum), cd=jnp.exp(gl)))
    lmat_f, lmat, attn = {}, {}, {}
    for (c, h) in items:
        cs = slice(h * GDN_HD, (h + 1) * GDN_HD)
        pc = per_chunk[c]
        lm = pc['lm'][:, cs]
        kh = pc['k'][:, cs]
        lmat_f[c, h] = jnp.where(strict, _dot(pc['kb'][:, cs], kh, _NT) * lm, 0.0)
        lmat[c, h] = _split(lmat_f[c, h])
        attn[c, h] = jnp.where(incl, _dot(pc['q'][:, cs], kh, _NT) * lm, 0.0)
    tinv = {n: _split(jnp.where(eye, 1.0, 0.0) - jnp.where(lev_masks[0], lmat_f[n], 0.0)) for n in items}
    for m in lev_masks[1:]:
        xs = {n: _split(_dot3((jnp.where(m, lmat[n][0], jnp.zeros_like(lmat[n][0])),
                               jnp.where(m, lmat[n][1], jnp.zeros_like(lmat[n][1]))), tinv[n])) for n in items}
        tinv = {n: _split(tinv[n][0].astype(f32) + tinv[n][1].astype(f32) - _dot3(tinv[n], xs[n])) for n in items}
    u, w, mm, nn, pp, rr = {}, {}, {}, {}, {}, {}
    for (c, h) in items:
        cs = slice(h * GDN_HD, (h + 1) * GDN_HD)
        pc = per_chunk[c]
        u[c, h] = _dot3(tinv[c, h], _split(pc['vb'][:, cs]))
        w[c, h] = _dot3(tinv[c, h], _split(pc['kbe'][:, cs]))
    for (c, h) in items:
        cs = slice(h * GDN_HD, (h + 1) * GDN_HD)
        pc = per_chunk[c]
        kd = _split(pc['kd'][:, cs])
        mm[c, h] = _split(jnp.where(eye, pc['cd'][:, cs], 0.0) - _dot3(kd, _split(w[c, h]), _TN))
        nn[c, h] = _dot3(kd, _split(u[c, h]), _TN)
        pp[c, h] = _dot(attn[c, h], u[c, h])
        rr[c, h] = _split(pc['qd'][:, cs] - _dot(attn[c, h], w[c, h]))
    for c in range(n_chunks):
        r0 = c * c_
        for h in range(GDN_HEADS):
            cs = slice(h * GDN_HD, (h + 1) * GDN_HD)
            st = _split(state_ref[h])
            o_scr[r0:r0 + c_, cs] = pp[c, h] + _dot3(rr[c, h], st)
            state_ref[h] = _dot3(mm[c, h], st) + nn[c, h]
    o = o_scr[...]
    on = o * lax.rsqrt(_dot_sel(o * o, bones) * (1.0 / GDN_HD) + EPS) * nw_ref[...]
    g = x_ref[:, 768:1024]
    yy = on * (g * jax.nn.sigmoid(g))
    out_ref[...] = _rms(yy, bn_ref[...])


def _gdn(gdn_in, ba, cw, alog, dtb, eb, ea, bones, nw, bn):
    s = gdn_in.shape[0]
    tb = min(s, TB_GDN)
    row = _full((1, GROUP_W))
    return pl.pallas_call(
        _gdn_kernel, grid=(s // tb,),
        in_specs=[_rows(tb, 1024), _rows(tb, 128), _full((CONV_K, 3 * GROUP_W)), _full((1, 128)), _full((1, 128)),
                  _full((128, GROUP_W)), _full((128, GROUP_W)), _full((GROUP_W, GROUP_W)), row, row],
        out_specs=_rows(tb, GROUP_W),
        out_shape=jax.ShapeDtypeStruct((s, GROUP_W), f32),
        scratch_shapes=[pltpu.VMEM((tb + 8, 3 * GROUP_W), f32), pltpu.VMEM((GDN_HEADS, GDN_HD, GDN_HD), f32),
                        pltpu.VMEM((tb, GROUP_W), f32)],
        compiler_params=_params(), name="gated_deltanet")(gdn_in, ba, cw, alog, dtb, eb, ea, bones, nw, bn)


def _out_kernel(h_ref, m0_ref, m1_ref, m2_ref, m3_ref, w_ref, out_ref):
    acc = h_ref[...]
    for j, m in enumerate((m0_ref, m1_ref, m2_ref, m3_ref)):
        acc = acc + jnp.dot(m[...].astype(bf16), w_ref[j * GROUP_W:(j + 1) * GROUP_W, :], preferred_element_type=f32)
    out_ref[...] = acc


def _out_proj(h, mixed, w):
    s = h.shape[0]
    tb = min(s, TB_OUT)
    return pl.pallas_call(
        _out_kernel, grid=(s // tb,),
        in_specs=[_rows(tb, D_MODEL)] + [_rows(tb, GROUP_W)] * 4 + [_full((D_MODEL, D_MODEL))],
        out_specs=_rows(tb, D_MODEL),
        out_shape=jax.ShapeDtypeStruct((s, D_MODEL), f32),
        compiler_params=_params(), name="out_proj")(h, *mixed, w)


def _extract_topk(s, idx_col, n_bad, put):
    for r in range(PEER_TOPK):
        m = jnp.max(s, axis=0, keepdims=True)
        idx = jnp.min(jnp.where(s == m, idx_col, n_bad), axis=0, keepdims=True)
        oh = idx_col == idx
        put(r, m, oh)
        s = jnp.where(oh, -jnp.inf, s)


def _extract_topk_distinct(s, put):
    for r in range(PEER_TOPK):
        m = jnp.max(s, axis=0, keepdims=True)
        oh = s == m
        put(r, m, oh)
        s = jnp.where(oh, -jnp.inf, s)


def _any_count_off(marks):
    n = jnp.sum(marks, axis=0, keepdims=True)
    return jnp.max(jnp.abs(n - float(PEER_TOPK))) > 0.5


def _topk_kernel(h_ref, nw_ref, wqt_ref, sk_ref, p1_ref, p2_ref, cidx_ref, qsel_ref,
                 zt_ref, r2_ref, jc_ref, e1_ref, e2_ref, s_scr, rank_scr, val_scr, sel_scr):
    tb = h_ref.shape[0]
    z = _rms(h_ref[...], nw_ref[...])
    zt = z.T.astype(bf16)
    zt_ref[...] = zt
    qt = jnp.dot(wqt_ref[...], zt, preferred_element_type=f32)
    for hp in range(2 * PEER_HEADS):
        s_scr[hp] = _dot(sk_ref[hp], qt[hp * PEER_HALF:(hp + 1) * PEER_HALF, :])

    key_idx = lax.broadcasted_iota(jnp.int32, (PEER_NKEYS, tb), 0).astype(f32)

    def stage1(h, carry):
        hps = (2 * h, 2 * h + 1)
        ranks = [jnp.full((PEER_NKEYS, tb), float(PEER_TOPK), f32) for _ in hps]
        ss = [s_scr[hp] for hp in hps]
        for r in range(PEER_TOPK):
            ms = [jnp.max(s, axis=0, keepdims=True) for s in ss]
            ohs = [s == m for s, m in zip(ss, ms)]
            for i, hp in enumerate(hps):
                val_scr[hp, r:r + 1, :] = ms[i]
                ranks[i] = jnp.where(ohs[i], float(r), ranks[i])
            ss = [jnp.where(oh, -jnp.inf, s) for s, oh in zip(ss, ohs)]
        for i, hp in enumerate(hps):
            rank_scr[hp] = ranks[i]
        marks = jnp.concatenate([jnp.where(rk < float(PEER_TOPK), 1.0, 0.0) for rk in ranks], axis=1)

        @pl.when(_any_count_off(marks))
        def _():
            for hp in hps:
                rank = [jnp.full((PEER_NKEYS, tb), float(PEER_TOPK), f32)]

                def put(r, m, oh, hp=hp, rank=rank):
                    val_scr[hp, r:r + 1, :] = m
                    rank[0] = jnp.where(oh, float(r), rank[0])

                _extract_topk(s_scr[hp], key_idx, float(PEER_NKEYS), put)
                rank_scr[hp] = rank[0]

        return carry

    lax.fori_loop(0, PEER_HEADS, stage1, 0)

    cidx = jnp.broadcast_to(cidx_ref[...], (PEER_NCAND, tb))
    p1 = p1_ref[...]
    p2 = p2_ref[...]

    def stage2(h, carry):
        v1 = val_scr[2 * h]
        v2 = val_scr[2 * h + 1]
        cand = _dot(p1, v1, hi=True) + _dot(p2, v2, hi=True)
        cand = jnp.where(cidx < 256.0, cand, -jnp.inf)
        def run(extract):
            sel = [jnp.zeros((PEER_NCAND, tb), f32)]

            def put(r, m, oh):
                sel[0] = jnp.where(oh, 1.0, sel[0])

            extract(cand, put)
            sel_scr[...] = sel[0]
            return sel[0]

        sel_fast = run(_extract_topk_distinct)

        @pl.when(_any_count_off(sel_fast))
        def _():
            run(lambda s, put: _extract_topk(s, cidx, 1e9, put))

        sel = sel_scr[...]
        m1 = v1[0:1, :]
        m2 = v2[0:1, :]
        wgt = sel * _dot_sel_lhs(p1.astype(bf16), jnp.exp(v1 - m1)) * _dot_sel_lhs(p2.astype(bf16), jnp.exp(v2 - m2))
        zsum = jnp.sum(wgt, axis=0, keepdims=True)
        cnt = _bdot(qsel_ref[...].astype(bf16), sel.astype(bf16))
        r1 = rank_scr[2 * h]
        r2 = rank_scr[2 * h + 1]
        jc = jnp.zeros((PEER_NKEYS, tb), f32)
        for i in range(PEER_TOPK):
            jc = jnp.where(r1 == float(i), cnt[i:i + 1, :], jc)
        r2_ref[h] = r2.astype(bf16)
        jc_ref[h] = jc
        e1_ref[h] = jnp.where(r1 < float(PEER_TOPK), jnp.exp(s_scr[2 * h] - m1), 0.0)
        e2_ref[h] = (jnp.where(r2 < float(PEER_TOPK), jnp.exp(s_scr[2 * h + 1] - m2), 0.0) / zsum).astype(bf16)
        return carry

    lax.fori_loop(0, PEER_HEADS, stage2, 0)


def _peer_topk(h, nw, wqt, sk, p1, p2, cidx, qsel):
    s = h.shape[0]
    tb = min(s, TB_TOPK)
    gate_spec = pl.BlockSpec((PEER_HEADS, PEER_NKEYS, tb), lambda i: (0, 0, i))
    gate_shape = jax.ShapeDtypeStruct((PEER_HEADS, PEER_NKEYS, s), f32)
    gate_shape16 = jax.ShapeDtypeStruct((PEER_HEADS, PEER_NKEYS, s), bf16)
    return pl.pallas_call(
        _topk_kernel, grid=(s // tb,),
        in_specs=[_rows(tb, D_MODEL), _full((1, D_MODEL)), _full((D_MODEL, D_MODEL)),
                  _full((2 * PEER_HEADS, PEER_NKEYS, PEER_HALF)), _full((PEER_NCAND, PEER_TOPK)),
                  _full((PEER_NCAND, PEER_TOPK)), _full((PEER_NCAND, 1)), _full((PEER_TOPK, PEER_NCAND))],
        out_specs=[pl.BlockSpec((D_MODEL, tb), lambda i: (0, i)), gate_spec, gate_spec, gate_spec, gate_spec],
        out_shape=[jax.ShapeDtypeStruct((D_MODEL, s), bf16), gate_shape16, gate_shape, gate_shape, gate_shape16],
        scratch_shapes=[pltpu.VMEM((2 * PEER_HEADS, PEER_NKEYS, tb), f32),
                        pltpu.VMEM((2 * PEER_HEADS, PEER_NKEYS, tb), f32),
                        pltpu.VMEM((2 * PEER_HEADS, PEER_TOPK, tb), f32), pltpu.VMEM((PEER_NCAND, tb), f32)],
        compiler_params=_params(), name="peer_topk")(h, nw, wqt, sk, p1, p2, cidx, qsel)


def _peer_kernel(zt_ref, h_ref, u_ref, vt_ref, r2_ref, jc_ref, e1_ref, e2_ref, out_ref, acc_ref, act_ref, hh_ref):
    j = pl.program_id(1)
    tb = h_ref.shape[0]
    ne = u_ref.shape[0]
    na = ne // PEER_NKEYS
    n_tiles = pl.num_programs(1) - 2

    @pl.when(j == 0)
    def _():
        acc_ref[...] = jnp.zeros_like(acc_ref)
        act_ref[...] = jnp.zeros_like(act_ref)
        hh_ref[...] = jnp.zeros_like(hh_ref)

    cur = j % 2
    prev = 1 - cur
    tile = jnp.clip(j - 1, 0, n_tiles - 1)
    n_mm = na
    rows_v = acc_ref.shape[0] // n_mm
    rows_u = ne // n_mm
    for ai in range(na):
        mi = ai * n_mm // na
        if ai * n_mm % na == 0:
            v0 = mi * rows_v
            acc_ref[v0:v0 + rows_v, :] += jnp.dot(vt_ref[v0:v0 + rows_v, :], hh_ref[cur],
                                                   preferred_element_type=f32)
        a = tile * na + ai
        g = jnp.zeros((PEER_NKEYS, tb), bf16)
        for h in range(PEER_HEADS):
            jc = jc_ref[h, pl.ds(a, 1), :].astype(bf16)
            e1 = e1_ref[h, pl.ds(a, 1), :].astype(bf16)
            g = g + jnp.where(r2_ref[h] < jc, e2_ref[h] * e1, jnp.zeros_like(g))
        r0 = ai * PEER_NKEYS
        hh_ref[prev, r0:r0 + PEER_NKEYS, :] = g * jax.nn.gelu(act_ref[prev, r0:r0 + PEER_NKEYS, :].astype(bf16))
        if (ai + 1) * n_mm % na == 0:
            u0 = mi * rows_u
            act_ref[cur, u0:u0 + rows_u, :] = jnp.dot(u_ref[u0:u0 + rows_u, :], zt_ref[...],
                                                      preferred_element_type=f32)

    @pl.when(j == pl.num_programs(1) - 1)
    def _():
        out_ref[...] = h_ref[...] + acc_ref[...].T


def _peer_dense(zt, h, u, vt, r2, jc, e1, e2):
    s = h.shape[0]
    tb = min(s, TB_PEER)
    n_tiles = u.shape[0] // NE_TILE
    gate_spec = pl.BlockSpec((PEER_HEADS, PEER_NKEYS, tb), lambda i, j: (0, 0, i))
    return pl.pallas_call(
        _peer_kernel, grid=(s // tb, n_tiles + 2),
        in_specs=[pl.BlockSpec((D_MODEL, tb), lambda i, j: (0, i)), pl.BlockSpec((tb, D_MODEL), lambda i, j: (i, 0)),
                  pl.BlockSpec((NE_TILE, D_MODEL), lambda i, j: (jnp.minimum(j, n_tiles - 1), 0)),
                  pl.BlockSpec((None, D_MODEL, NE_TILE), lambda i, j: (jnp.clip(j - 2, 0, n_tiles - 1), 0, 0)),
                  gate_spec, gate_spec, gate_spec, gate_spec],
        out_specs=pl.BlockSpec((tb, D_MODEL), lambda i, j: (i, 0)),
        out_shape=jax.ShapeDtypeStruct((s, D_MODEL), f32),
        scratch_shapes=[pltpu.VMEM((D_MODEL, tb), f32), pltpu.VMEM((2, NE_TILE, tb), f32),
                        pltpu.VMEM((2, NE_TILE, tb), bf16)],
        compiler_params=_params(2), name="peer_dense")(zt, h, u, vt, r2, jc, e1, e2)


def _ple_kernel(h_ref, p_ref, nw_ref, wg_ref, wp_ref, fw_ref, out_ref, *, final):
    x = h_ref[...]
    gate = jax.nn.sigmoid(jnp.dot(_rms(x, nw_ref[...]).astype(bf16), wg_ref[...], preferred_element_type=f32))
    y = x + jnp.dot(p_ref[...].astype(bf16), wp_ref[...], preferred_element_type=f32) * gate
    if final:
        y = _rms(y, fw_ref[...])
    out_ref[...] = y


def _ple(h, p, nw, wg, wp, fw, final):
    s = h.shape[0]
    tb = min(s, TB_PLE)
    pw = p.shape[1]
    return pl.pallas_call(
        functools.partial(_ple_kernel, final=final), grid=(s // tb,),
        in_specs=[_rows(tb, D_MODEL), _rows(tb, pw), _full((1, D_MODEL)), _full((D_MODEL, D_MODEL)),
                  _full((pw, D_MODEL)), _full((1, D_MODEL))],
        out_specs=_rows(tb, D_MODEL),
        out_shape=jax.ShapeDtypeStruct((s, D_MODEL), f32),
        compiler_params=_params(), name="ple_gate")(h, p, nw, wg, wp, fw)


def _block_diag(blocks):
    n, r, c = blocks.shape
    return jnp.einsum('nrc,nm->nrmc', blocks, jnp.eye(n, dtype=blocks.dtype)).reshape(n * r, n * c)


def _retention_consts():
    log_gamma = jnp.log(1.0 - 2.0 ** (-5.0 - jnp.arange(RET_HEADS, dtype=f32)))
    idx = jnp.arange(RET_CHUNK, dtype=f32)
    diff = idx[:, None] - idx[None, :]
    causal = diff >= 0
    dmask = jnp.where(causal, jnp.exp(log_gamma[:, None, None] * jnp.where(causal, diff, 0.0)), 0.0)
    q_decay = jnp.exp(log_gamma[:, None] * (idx + 1.0))
    k_decay = jnp.exp(log_gamma[:, None] * (RET_CHUNK - 1.0 - idx))
    chunk_decay = jnp.exp(log_gamma * RET_CHUNK)
    qdec = jnp.repeat(q_decay.T, RET_HD, axis=1)
    kdec = jnp.repeat(k_decay.T, RET_HD, axis=1)
    bmask = _block_diag(jnp.ones((RET_HEADS, RET_HD, RET_HD), f32))
    cdec = _block_diag(jnp.broadcast_to(chunk_decay[:, None, None], (RET_HEADS, RET_HD, RET_HD)))
    return dmask, qdec, kdec, cdec, bmask, bmask


def _s5_discretize(a_re, a_im, b_re, b_im, c_re, c_im, log_dt):
    dt = jnp.exp(log_dt)[:, None]
    mag = jnp.exp(a_re * dt)
    ang = a_im * dt
    ab_re, ab_im = mag * jnp.cos(ang), mag * jnp.sin(ang)
    den = a_re * a_re + a_im * a_im
    p_re, p_im = ab_re - 1.0, ab_im
    f_re = (p_re * a_re + p_im * a_im) / den
    f_im = (p_im * a_re - p_re * a_im) / den
    bb_re = f_re[..., None] * b_re - f_im[..., None] * b_im
    bb_im = f_re[..., None] * b_im + f_im[..., None] * b_re
    abar = jnp.stack([ab_re.reshape(-1), ab_im.reshape(-1)])
    bre = _block_diag(jnp.swapaxes(bb_re, 1, 2))
    bim = _block_diag(jnp.swapaxes(bb_im, 1, 2))
    cre = _block_diag(jnp.swapaxes(c_re, 1, 2))
    cim = _block_diag(jnp.swapaxes(c_im, 1, 2))
    return abar, bre, bim, cre, cim


def _peer_rank_consts():
    pairs = [(i, j) for i in range(PEER_TOPK) for j in range(PEER_TOPK) if (i + 1) * (j + 1) <= PEER_TOPK]
    p1 = np.zeros((PEER_NCAND, PEER_TOPK), np.float32)
    p2 = np.zeros((PEER_NCAND, PEER_TOPK), np.float32)
    cidx = np.full((PEER_NCAND, 1), 1e9, np.float32)
    for c, (i, j) in enumerate(pairs):
        p1[c, i] = 1.0
        p2[c, j] = 1.0
        cidx[c, 0] = PEER_TOPK * i + j
    return jnp.asarray(p1), jnp.asarray(p2), jnp.asarray(cidx), jnp.asarray(p1.T)


def _row(v):
    return v.reshape(1, -1).astype(f32)


def kernel(x, p, positions, mix_norm, w_in, ret_gn, lru_conv_w, lru_conv_b, lru_w_a, lru_b_a, lru_w_x, lru_b_x, lru_lambda, s5_a_re, s5_a_im, s5_b_re, s5_b_im, s5_c_re, s5_c_im, s5_d, s5_log_dt, s5_glu_w, s5_glu_b, gdn_conv_w, gdn_a_log, gdn_dt_bias, gdn_norm, branch_norm, w_out, ffn_norm, peer_wq, peer_subkeys, peer_u, peer_v, ple_norm, ple_wg, ple_wp, final_norm):
    bsz, seq, _ = x.shape
    depth = w_in.shape[0]
    assert bsz == 1
    h = x.reshape(seq, D_MODEL)

    half = RET_HD // 2
    inv_freq = ROPE_BASE ** (-jnp.arange(half, dtype=f32) / half)
    freq = jnp.tile(inv_freq, 128 // half).reshape(1, 128)
    cos, sin = _rope_tables(positions.reshape(seq, 1).astype(f32), freq)

    ret_consts = _retention_consts()
    bones = ret_consts[5]
    p1, p2, cidx, qsel = _peer_rank_consts()
    head_rows = jnp.arange(128)[:, None]
    head_lanes = jnp.arange(GROUP_W)[None, :] // GDN_HD
    eb = (head_rows == head_lanes).astype(f32)
    ea = (head_rows == head_lanes + GDN_HEADS).astype(f32)

    for l in range(depth):
        w = w_in[l]
        w_perm = jnp.concatenate([w[:, 0:2560], w[:, 2568:2824], w[:, 2560:2568],
                                  jnp.zeros((D_MODEL, IN_PAD - 2824), f32)], axis=1).astype(bf16)
        ret_in, lru_in, s5_in, gdn_in, ba_in = _in_proj(h, _row(mix_norm[l]), w_perm)

        y_ret = _retention(ret_in, cos, sin, ret_consts, _row(ret_gn[l]), _row(branch_norm[l, 0]))
        y_lru = _rglru(lru_in, lru_conv_w[l], _row(lru_conv_b[l]), _block_diag(lru_w_a[l]), _row(lru_b_a[l]),
                       _block_diag(lru_w_x[l]), _row(lru_b_x[l]), _row(lru_lambda[l]), _row(branch_norm[l, 1]))
        abar, bre, bim, cre, cim = _s5_discretize(s5_a_re[l], s5_a_im[l], s5_b_re[l], s5_b_im[l],
                                                  s5_c_re[l], s5_c_im[l], s5_log_dt[l])
        y_s5 = _s5(s5_in, abar, bre, bim, cre, cim, _row(s5_d[l]), s5_glu_w[l], _row(s5_glu_b[l]),
                   _row(branch_norm[l, 2]))
        alog = jnp.zeros((1, 128), f32).at[0, GDN_HEADS:2 * GDN_HEADS].set(gdn_a_log[l])
        dtb = jnp.zeros((1, 128), f32).at[0, GDN_HEADS:2 * GDN_HEADS].set(gdn_dt_bias[l])
        y_gdn = _gdn(gdn_in, ba_in, gdn_conv_w[l], alog, dtb, eb, ea, bones,
                     _row(jnp.tile(gdn_norm[l], GDN_HEADS)), _row(branch_norm[l, 3]))

        h = _out_proj(h, (y_ret, y_lru, y_s5, y_gdn), w_out[l].astype(bf16))

        sk = peer_subkeys[l].reshape(2 * PEER_HEADS, PEER_NKEYS, PEER_HALF)
        zt, r2, jc, e1, e2 = _peer_topk(h, _row(ffn_norm[l]), peer_wq[l].T.astype(bf16), sk, p1, p2, cidx, qsel)
        vt_tiles = peer_v[l].astype(bf16).reshape(-1, NE_TILE, D_MODEL).swapaxes(1, 2)
        h = _peer_dense(zt, h, peer_u[l].astype(bf16), vt_tiles, r2, jc, e1, e2)

        h = _ple(h, p[l].reshape(seq, -1), _row(ple_norm[l]), ple_wg[l].astype(bf16), ple_wp[l].astype(bf16),
                 _row(final_norm), final=(l == depth - 1))
    return h.reshape(bsz, seq, D_MODEL)
```

```python
import functools
import math

import jax
import jax.numpy as jnp
import numpy as np
from jax import lax
from jax.experimental import pallas as pl
from jax.experimental.pallas import tpu as pltpu

f32 = jnp.float32
bf16 = jnp.bfloat16
_HI = lax.Precision.HIGHEST
_NT = (((1,), (1,)), ((), ()))
_TN = (((0,), (0,)), ((), ()))

EPS = 1e-6
D_MODEL = 1024
GROUP_W = 256
CONV_K = 4
RET_HEADS = 4
RET_HD = 64
RET_CHUNK = 128
ROPE_BASE = 10000.0
LRU_C = 8.0
S5_GROUPS = 16
S5_GW = 16
S5_STATE = 64
S5_N = S5_GROUPS * S5_STATE
S5_TILE = 8
GDN_HEADS = 4
GDN_HD = 64
GDN_CHUNK = 64
PEER_HEADS = 8
PEER_NKEYS = 128
PEER_HALF = 64
PEER_TOPK = 16
PEER_NCAND = 64

IN_PAD = 2944
VMEM_LIMIT = 48 * 1024 * 1024

TB_IN = 256
TB_RET = 512
TB_LRU = 512
TB_S5 = 512
TB_GDN = 256
TB_OUT = 512
TB_TOPK = 256
TB_PEER = 512
NE_TILE = 512
TB_PLE = 512


def _dot(a, b, dims=(((1,), (0,)), ((), ())), hi=False):
    if hi:
        return lax.dot_general(a, b, dims, precision=_HI, preferred_element_type=f32)
    return lax.dot_general(a.astype(bf16), b.astype(bf16), dims, preferred_element_type=f32)


def _split(a):
    hi = a.astype(bf16)
    return hi, (a - hi.astype(f32)).astype(bf16)


def _bdot(a, b, dims=(((1,), (0,)), ((), ()))):
    return lax.dot_general(a, b, dims, preferred_element_type=f32)


def _dot3(a, b, dims=(((1,), (0,)), ((), ()))):
    (ah, al), (bh, bl) = a, b
    return _bdot(ah, bh, dims) + (_bdot(ah, bl, dims) + _bdot(al, bh, dims))


def _dot_sel(x, sel):
    xh, xl = _split(x)
    s = sel.astype(bf16)
    return _bdot(xh, s) + _bdot(xl, s)


def _dot_sel_lhs(sel, x):
    xh, xl = _split(x)
    return _bdot(sel, xh) + _bdot(sel, xl)


def _rms(x, w):
    return x * lax.rsqrt(jnp.mean(x * x, axis=-1, keepdims=True) + EPS) * w


def _shift_rows(x, d, fill):
    row = lax.broadcasted_iota(jnp.int32, x.shape, 0)
    return jnp.where(row >= d, pltpu.roll(x, d, 0), fill)


def _params(n_axes=1):
    return pltpu.CompilerParams(dimension_semantics=("arbitrary",) * n_axes, vmem_limit_bytes=VMEM_LIMIT)


def _full(shape):
    n = len(shape)
    return pl.BlockSpec(shape, lambda *_: (0,) * n)


def _rows(tb, w):
    return pl.BlockSpec((tb, w), lambda i: (i, 0))


def _rope_kernel(pos_ref, freq_ref, cos_ref, sin_ref):
    ang = pos_ref[...] * freq_ref[...]
    cos_ref[...] = jnp.cos(ang)
    sin_ref[...] = jnp.sin(ang)


def _rope_tables(pos, freq):
    s = pos.shape[0]
    tb = min(s, 1024)
    return pl.pallas_call(
        _rope_kernel, grid=(s // tb,),
        in_specs=[_rows(tb, 1), _full((1, 128))],
        out_specs=[_rows(tb, 128), _rows(tb, 128)],
        out_shape=[jax.ShapeDtypeStruct((s, 128), f32)] * 2,
        compiler_params=_params(), name="rope_tables")(pos, freq)


def _in_kernel(h_ref, nw_ref, w_ref, ret_ref, lru_ref, s5_ref, gdn_ref, ba_ref):
    z = _rms(h_ref[...], nw_ref[...]).astype(bf16)
    off = 0
    for ref in (ret_ref, lru_ref, s5_ref, gdn_ref, ba_ref):
        w = ref.shape[1]
        ref[...] = jnp.dot(z, w_ref[:, off:off + w], preferred_element_type=f32)
        off += w


def _in_proj(h, nw, w):
    s = h.shape[0]
    tb = min(s, TB_IN)
    widths = (1024, 512, 256, 1024, 128)
    return pl.pallas_call(
        _in_kernel, grid=(s // tb,),
        in_specs=[_rows(tb, D_MODEL), _full((1, D_MODEL)), _full((D_MODEL, IN_PAD))],
        out_specs=[_rows(tb, w_) for w_ in widths],
        out_shape=[jax.ShapeDtypeStruct((s, w_), f32) for w_ in widths],
        compiler_params=_params(), name="in_proj")(h, nw, w)


def _ret_kernel(x_ref, cos_ref, sin_ref, dmask_ref, qdec_ref, kdec_ref, cdec_ref, bmask_ref, bones_ref,
                gn_ref, bn_ref, out_ref, state_ref, o_scr):
    tb = x_ref.shape[0]

    @pl.when(pl.program_id(0) == 0)
    def _():
        state_ref[...] = jnp.zeros_like(state_ref)

    lane = lax.broadcasted_iota(jnp.int32, (1, GROUP_W), 1)
    first = (lane % RET_HD) < (RET_HD // 2)
    cosf = jnp.concatenate([cos_ref[...], cos_ref[...]], axis=1)
    sinf = jnp.concatenate([sin_ref[...], sin_ref[...]], axis=1) * jnp.where(first, -1.0, 1.0)

    def rot(t):
        partner = jnp.where(first, pltpu.roll(t, GROUP_W - RET_HD // 2, 1), pltpu.roll(t, RET_HD // 2, 1))
        return t * cosf + partner * sinf

    q = rot(x_ref[:, 0:256]) * (RET_HD ** -0.5)
    k = rot(x_ref[:, 256:512])
    v = x_ref[:, 512:768]
    qdec = qdec_ref[...]
    kdec = kdec_ref[...]
    for c in range(tb // RET_CHUNK):
        r0 = c * RET_CHUNK
        qc, kc, vc = q[r0:r0 + RET_CHUNK], k[r0:r0 + RET_CHUNK], v[r0:r0 + RET_CHUNK]
        st = state_ref[...]
        o = _dot(qc, st) * qdec
        for h in range(RET_HEADS):
            hm = (lane // RET_HD) == h
            sc = _dot(jnp.where(hm, qc, 0.0), kc, _NT) * dmask_ref[h]
            o = o + _dot(sc, jnp.where(hm, vc, 0.0))
        kv = _dot(kc * kdec, vc, _TN)
        state_ref[...] = st * cdec_ref[...] + kv * bmask_ref[...]
        o_scr[r0:r0 + RET_CHUNK, :] = o
    o = o_scr[...]
    bones = bones_ref[...]
    mu = _dot(o, bones, hi=True) * (1.0 / RET_HD)
    d = o - mu
    var = _dot(d * d, bones, hi=True) * (1.0 / RET_HD)
    on = d * lax.rsqrt(var + EPS) * gn_ref[...]
    g = x_ref[:, 768:1024]
    y = g * jax.nn.sigmoid(g) * on
    out_ref[...] = _rms(y, bn_ref[...])


def _retention(ret_in, cos, sin, consts, gn, bn):
    s = ret_in.shape[0]
    tb = min(s, TB_RET)
    dmask, qdec, kdec, cdec, bmask, bones = consts
    return pl.pallas_call(
        _ret_kernel, grid=(s // tb,),
        in_specs=[_rows(tb, 1024), _rows(tb, 128), _rows(tb, 128), _full(dmask.shape), _full(qdec.shape),
                  _full(kdec.shape), _full(cdec.shape), _full(bmask.shape), _full(bones.shape),
                  _full((1, GROUP_W)), _full((1, GROUP_W))],
        out_specs=_rows(tb, GROUP_W),
        out_shape=jax.ShapeDtypeStruct((s, GROUP_W), f32),
        scratch_shapes=[pltpu.VMEM((GROUP_W, GROUP_W), f32), pltpu.VMEM((tb, GROUP_W), f32)],
        compiler_params=_params(), name="retention")(ret_in, cos, sin, dmask, qdec, kdec, cdec, bmask, bones, gn, bn)


def _lru_kernel(x_ref, cw_ref, cb_ref, wa_ref, ba_ref, wx_ref, bx_ref, lam_ref, bn_ref, out_ref, xbuf, carry_ref):
    tb = x_ref.shape[0]

    @pl.when(pl.program_id(0) == 0)
    def _():
        xbuf[0:8, :] = jnp.zeros((8, GROUP_W), f32)
        carry_ref[...] = jnp.zeros_like(carry_ref)

    x = x_ref[:, 256:512]
    xbuf[8:8 + tb, :] = x
    cw = cw_ref[...]
    xb = (cw[3:4] * x + cw[2:3] * xbuf[7:7 + tb, :] + cw[1:2] * xbuf[6:6 + tb, :]
          + cw[0:1] * xbuf[5:5 + tb, :]) + cb_ref[...]
    xbuf[0:8, :] = xbuf[tb:tb + 8, :]
    r = jax.nn.sigmoid(_dot(xb, wa_ref[...], hi=True) + ba_ref[...])
    i = jax.nn.sigmoid(_dot(xb, wx_ref[...], hi=True) + bx_ref[...])
    log_a = -LRU_C * r * jax.nn.softplus(-lam_ref[...])
    a = jnp.exp(log_a)
    y2 = 2.0 * log_a
    one_minus_a2 = -jnp.tanh(0.5 * y2) * (jnp.exp(y2) + 1.0)
    b = jnp.sqrt(one_minus_a2) * (i * xb)
    d = 1
    while d < tb:
        b = a * _shift_rows(b, d, 0.0) + b
        a = a * _shift_rows(a, d, 1.0)
        d *= 2
    h = b + a * carry_ref[...]
    carry_ref[...] = h[tb - 1:tb, :]
    y = jax.nn.gelu(x_ref[:, 0:256]) * h
    out_ref[...] = _rms(y, bn_ref[...])


def _rglru(lru_in, cw, cb, wa, ba, wx, bx, lam, bn):
    s = lru_in.shape[0]
    tb = min(s, TB_LRU)
    row = _full((1, GROUP_W))
    sq = _full((GROUP_W, GROUP_W))
    return pl.pallas_call(
        _lru_kernel, grid=(s // tb,),
        in_specs=[_rows(tb, 512), _full((CONV_K, GROUP_W)), row, sq, row, sq, row, row, row],
        out_specs=_rows(tb, GROUP_W),
        out_shape=jax.ShapeDtypeStruct((s, GROUP_W), f32),
        scratch_shapes=[pltpu.VMEM((tb + 8, GROUP_W), f32), pltpu.VMEM((1, GROUP_W), f32)],
        compiler_params=_params(), name="rglru")(lru_in, cw, cb, wa, ba, wx, bx, lam, bn)


def _s5_kernel(u_ref, a_ref, bre_ref, bim_ref, cre_ref, cim_ref, d_ref, gw_ref, gb_ref, bn_ref, out_ref, carry_ref,
               xr_scr, xi_scr):
    tb = u_ref.shape[0]

    @pl.when(pl.program_id(0) == 0)
    def _():
        carry_ref[...] = jnp.zeros_like(carry_ref)

    u = u_ref[...]
    ar = a_ref[0:1, :]
    ai = a_ref[1:2, :]
    xr = _dot(u, bre_ref[...])
    xi = _dot(u, bim_ref[...])
    row8 = lax.broadcasted_iota(jnp.int32, xr.shape, 0) % S5_TILE
    prow = lax.broadcasted_iota(jnp.int32, (S5_TILE, S5_N), 0)
    pr = jnp.broadcast_to(ar, (S5_TILE, S5_N))
    pi = jnp.broadcast_to(ai, (S5_TILE, S5_N))
    d = 1
    while d < S5_TILE:
        sr = jnp.where(row8 >= d, pltpu.roll(xr, d, 0), 0.0)
        si = jnp.where(row8 >= d, pltpu.roll(xi, d, 0), 0.0)
        xr, xi = xr + (ar * sr - ai * si), xi + (ar * si + ai * sr)
        qr = jnp.where(prow >= d, pltpu.roll(pr, d, 0), 1.0)
        qi = jnp.where(prow >= d, pltpu.roll(pi, d, 0), 0.0)
        pr, pi = pr * qr - pi * qi, pr * qi + pi * qr
        ar, ai = ar * ar - ai * ai, 2.0 * ar * ai
        d *= 2
    cr = carry_ref[0:1, :]
    ci = carry_ref[1:2, :]
    for t in range(tb // S5_TILE):
        r0 = t * S5_TILE
        tr = xr[r0:r0 + S5_TILE, :] + (pr * cr - pi * ci)
        ti = xi[r0:r0 + S5_TILE, :] + (pr * ci + pi * cr)
        xr_scr[r0:r0 + S5_TILE, :] = tr
        xi_scr[r0:r0 + S5_TILE, :] = ti
        cr = tr[S5_TILE - 1:S5_TILE, :]
        ci = ti[S5_TILE - 1:S5_TILE, :]
    carry_ref[0:1, :] = cr
    carry_ref[1:2, :] = ci
    y = _dot(xr_scr[...], cre_ref[...]) - _dot(xi_scr[...], cim_ref[...]) + d_ref[...] * u
    zz = _dot(jax.nn.gelu(y), gw_ref[...]) + gb_ref[...]
    o = zz[:, 0:GROUP_W] * jax.nn.sigmoid(zz[:, GROUP_W:2 * GROUP_W])
    out_ref[...] = _rms(o, bn_ref[...])


def _s5(s5_in, a, bre, bim, cre, cim, dd, gw, gb, bn):
    s = s5_in.shape[0]
    tb = min(s, TB_S5)
    row = _full((1, GROUP_W))
    return pl.pallas_call(
        _s5_kernel, grid=(s // tb,),
        in_specs=[_rows(tb, GROUP_W), _full((2, S5_N)), _full((GROUP_W, S5_N)), _full((GROUP_W, S5_N)),
                  _full((S5_N, GROUP_W)), _full((S5_N, GROUP_W)), row, _full((GROUP_W, 2 * GROUP_W)),
                  _full((1, 2 * GROUP_W)), row],
        out_specs=_rows(tb, GROUP_W),
        out_shape=jax.ShapeDtypeStruct((s, GROUP_W), f32),
        scratch_shapes=[pltpu.VMEM((2, S5_N), f32), pltpu.VMEM((tb, S5_N), f32), pltpu.VMEM((tb, S5_N), f32)],
        compiler_params=_params(), name="s5")(s5_in, a, bre, bim, cre, cim, dd, gw, gb, bn)


def _gdn_kernel(x_ref, ba_ref, cw_ref, alog_ref, dtb_ref, eb_ref, ea_ref, bones_ref, nw_ref, bn_ref, out_ref,
                xbuf, state_ref, o_scr):
    tb = x_ref.shape[0]
    c_ = GDN_CHUNK

    @pl.when(pl.program_id(0) == 0)
    def _():
        xbuf[0:8, :] = jnp.zeros((8, 3 * GROUP_W), f32)
        state_ref[...] = jnp.zeros_like(state_ref)

    x = x_ref[:, 0:768]
    xbuf[8:8 + tb, :] = x
    cw = cw_ref[...]
    y = cw[3:4] * x + cw[2:3] * xbuf[7:7 + tb, :] + cw[1:2] * xbuf[6:6 + tb, :] + cw[0:1] * xbuf[5:5 + tb, :]
    xbuf[0:8, :] = xbuf[tb:tb + 8, :]
    y = y * jax.nn.sigmoid(y)
    bones = bones_ref[...]
    q = y[:, 0:256]
    k = y[:, 256:512]
    v = y[:, 512:768]
    q = q * lax.rsqrt(_dot_sel(q * q, bones) + EPS) * (GDN_HD ** -0.5)
    k = k * lax.rsqrt(_dot_sel(k * k, bones) + EPS)
    ba = ba_ref[...]
    beta_b = _dot_sel(jax.nn.sigmoid(ba), eb_ref[...])
    la = -jnp.exp(alog_ref[...]) * jax.nn.softplus(ba + dtb_ref[...])
    la_b = _dot_sel(la, ea_ref[...])

    ri = lax.broadcasted_iota(jnp.int32, (c_, c_), 0)
    ci = lax.broadcasted_iota(jnp.int32, (c_, c_), 1)
    incl = ri >= ci
    strict = ri > ci
    eye = ri == ci
    ltri = incl.astype(bf16)
    ri4 = lax.broadcasted_iota(jnp.int32, (c_, GROUP_W), 0)
    ci4 = lax.broadcasted_iota(jnp.int32, (c_, GROUP_W), 1) % GDN_HD
    lev_masks = []
    b = 1
    while b < c_:
        lev_masks.append(((ri // (2 * b)) == (ci // (2 * b))) & (((ri // b) % 2) == 1) & (((ci // b) % 2) == 0))
        b *= 2

    n_chunks = tb // c_
    items = [(c, h) for c in range(n_chunks) for h in range(GDN_HEADS)]
    per_chunk = []
    for c in range(n_chunks):
        r0 = c * c_
        la_c = la_b[r0:r0 + c_, :]
        gcum = _dot_sel_lhs(ltri, la_c)
        dmat = _dot_sel_lhs(ltri, jnp.where(ri4 > ci4, la_c, 0.0))
        lm = jnp.where(ri4 >= ci4, jnp.exp(dmat), 0.0)
        eg = jnp.exp(gcum)
        gl = gcum[c_ - 1:c_, :]
        qc, kc, vc, bc = q[r0:r0 + c_, :], k[r0:r0 + c_, :], v[r0:r0 + c_, :], beta_b[r0:r0 + c_, :]
        kb = kc * bc
        per_chunk.append(dict(lm=lm, k=kc, q=qc, kb=kb, vb=vc * bc, kbe=kb * eg, qd=qc * eg,
                              kd=kc * jnp.exp(gl - gcum), cd=jnp.exp(gl)))
    lmat_f, lmat, attn = {}, {}, {}
    for (c, h) in items:
        cs = slice(h * GDN_HD, (h + 1) * GDN_HD)
        pc = per_chunk[c]
        lm = pc['lm'][:, cs]
        kh = pc['k'][:, cs]
        lmat_f[c, h] = jnp.where(strict, _dot(pc['kb'][:, cs], kh, _NT) * lm, 0.0)
        lmat[c, h] = _split(lmat_f[c, h])
        attn[c, h] = jnp.where(incl, _dot(pc['q'][:, cs], kh, _NT) * lm, 0.0)
    tinv = {n: _split(jnp.where(eye, 1.0, 0.0) - jnp.where(lev_masks[0], lmat_f[n], 0.0)) for n in items}
    for m in lev_masks[1:]:
        xs = {n: _split(_dot3((jnp.where(m, lmat[n][0], jnp.zeros_like(lmat[n][0])),
                               jnp.where(m, lmat[n][1], jnp.zeros_like(lmat[n][1]))), tinv[n])) for n in items}
        tinv = {n: _split(tinv[n][0].astype(f32) + tinv[n][1].astype(f32) - _dot3(tinv[n], xs[n])) for n in items}
    u, w, mm, nn, pp, rr = {}, {}, {}, {}, {}, {}
    for (c, h) in items:
        cs = slice(h * GDN_HD, (h + 1) * GDN_HD)
        pc = per_chunk[c]
        u[c, h] = _dot3(tinv[c, h], _split(pc['vb'][:, cs]))
        w[c, h] = _dot3(tinv[c, h], _split(pc['kbe'][:, cs]))
    for (c, h) in items:
        cs = slice(h * GDN_HD, (h + 1) * GDN_HD)
        pc = per_chunk[c]
        kd = _split(pc['kd'][:, cs])
        mm[c, h] = _split(jnp.where(eye, pc['cd'][:, cs], 0.0) - _dot3(kd, _split(w[c, h]), _TN))
        nn[c, h] = _dot3(kd, _split(u[c, h]), _TN)
        pp[c, h] = _dot(attn[c, h], u[c, h])
        rr[c, h] = _split(pc['qd'][:, cs] - _dot(attn[c, h], w[c, h]))
    for c in range(n_chunks):
        r0 = c * c_
        for h in range(GDN_HEADS):
            cs = slice(h * GDN_HD, (h + 1) * GDN_HD)
            st = _split(state_ref[h])
            o_scr[r0:r0 + c_, cs] = pp[c, h] + _dot3(rr[c, h], st)
            state_ref[h] = _dot3(mm[c, h], st) + nn[c, h]
    o = o_scr[...]
    on = o * lax.rsqrt(_dot_sel(o * o, bones) * (1.0 / GDN_HD) + EPS) * nw_ref[...]
    g = x_ref[:, 768:1024]
    yy = on * (g * jax.nn.sigmoid(g))
    out_ref[...] = _rms(yy, bn_ref[...])


def _gdn(gdn_in, ba, cw, alog, dtb, eb, ea, bones, nw, bn):
    s = gdn_in.shape[0]
    tb = min(s, TB_GDN)
    row = _full((1, GROUP_W))
    return pl.pallas_call(
        _gdn_kernel, grid=(s // tb,),
        in_specs=[_rows(tb, 1024), _rows(tb, 128), _full((CONV_K, 3 * GROUP_W)), _full((1, 128)), _full((1, 128)),
                  _full((128, GROUP_W)), _full((128, GROUP_W)), _full((GROUP_W, GROUP_W)), row, row],
        out_specs=_rows(tb, GROUP_W),
        out_shape=jax.ShapeDtypeStruct((s, GROUP_W), f32),
        scratch_shapes=[pltpu.VMEM((tb + 8, 3 * GROUP_W), f32), pltpu.VMEM((GDN_HEADS, GDN_HD, GDN_HD), f32),
                        pltpu.VMEM((tb, GROUP_W), f32)],
        compiler_params=_params(), name="gated_deltanet")(gdn_in, ba, cw, alog, dtb, eb, ea, bones, nw, bn)


def _out_kernel(h_ref, m0_ref, m1_ref, m2_ref, m3_ref, w_ref, out_ref):
    acc = h_ref[...]
    for j, m in enumerate((m0_ref, m1_ref, m2_ref, m3_ref)):
        acc = acc + jnp.dot(m[...].astype(bf16), w_ref[j * GROUP_W:(j + 1) * GROUP_W, :], preferred_element_type=f32)
    out_ref[...] = acc


def _out_proj(h, mixed, w):
    s = h.shape[0]
    tb = min(s, TB_OUT)
    return pl.pallas_call(
        _out_kernel, grid=(s // tb,),
        in_specs=[_rows(tb, D_MODEL)] + [_rows(tb, GROUP_W)] * 4 + [_full((D_MODEL, D_MODEL))],
        out_specs=_rows(tb, D_MODEL),
        out_shape=jax.ShapeDtypeStruct((s, D_MODEL), f32),
        compiler_params=_params(), name="out_proj")(h, *mixed, w)


def _extract_topk(s, idx_col, n_bad, put):
    for r in range(PEER_TOPK):
        m = jnp.max(s, axis=0, keepdims=True)
        idx = jnp.min(jnp.where(s == m, idx_col, n_bad), axis=0, keepdims=True)
        oh = idx_col == idx
        put(r, m, oh)
        s = jnp.where(oh, -jnp.inf, s)


def _extract_topk_distinct(s, put):
    for r in range(PEER_TOPK):
        m = jnp.max(s, axis=0, keepdims=True)
        oh = s == m
        put(r, m, oh)
        s = jnp.where(oh, -jnp.inf, s)


def _any_count_off(marks):
    n = jnp.sum(marks, axis=0, keepdims=True)
    return jnp.max(jnp.abs(n - float(PEER_TOPK))) > 0.5


def _topk_kernel(h_ref, nw_ref, wqt_ref, sk_ref, p1_ref, p2_ref, cidx_ref, qsel_ref,
                 zt_ref, r2_ref, jc_ref, e1_ref, e2_ref, s_scr, rank_scr, val_scr, sel_scr):
    tb = h_ref.shape[0]
    z = _rms(h_ref[...], nw_ref[...])
    zt = z.T.astype(bf16)
    zt_ref[...] = zt
    qt = jnp.dot(wqt_ref[...], zt, preferred_element_type=f32)
    for hp in range(2 * PEER_HEADS):
        s_scr[hp] = _dot(sk_ref[hp], qt[hp * PEER_HALF:(hp + 1) * PEER_HALF, :])

    key_idx = lax.broadcasted_iota(jnp.int32, (PEER_NKEYS, tb), 0).astype(f32)

    def stage1(h, carry):
        hps = (2 * h, 2 * h + 1)
        ranks = [jnp.full((PEER_NKEYS, tb), float(PEER_TOPK), f32) for _ in hps]
        ss = [s_scr[hp] for hp in hps]
        for r in range(PEER_TOPK):
            ms = [jnp.max(s, axis=0, keepdims=True) for s in ss]
            ohs = [s == m for s, m in zip(ss, ms)]
            for i, hp in enumerate(hps):
                val_scr[hp, r:r + 1, :] = ms[i]
                ranks[i] = jnp.where(ohs[i], float(r), ranks[i])
            ss = [jnp.where(oh, -jnp.inf, s) for s, oh in zip(ss, ohs)]
        for i, hp in enumerate(hps):
            rank_scr[hp] = ranks[i]
        marks = jnp.concatenate([jnp.where(rk < float(PEER_TOPK), 1.0, 0.0) for rk in ranks], axis=1)

        @pl.when(_any_count_off(marks))
        def _():
            for hp in hps:
                rank = [jnp.full((PEER_NKEYS, tb), float(PEER_TOPK), f32)]

                def put(r, m, oh, hp=hp, rank=rank):
                    val_scr[hp, r:r + 1, :] = m
                    rank[0] = jnp.where(oh, float(r), rank[0])

                _extract_topk(s_scr[hp], key_idx, float(PEER_NKEYS), put)
                rank_scr[hp] = rank[0]

        return carry

    lax.fori_loop(0, PEER_HEADS, stage1, 0)

    cidx = jnp.broadcast_to(cidx_ref[...], (PEER_NCAND, tb))
    p1 = p1_ref[...]
    p2 = p2_ref[...]

    def stage2(h, carry):
        v1 = val_scr[2 * h]
        v2 = val_scr[2 * h + 1]
        cand = _dot(p1, v1, hi=True) + _dot(p2, v2, hi=True)
        cand = jnp.where(cidx < 256.0, cand, -jnp.inf)
        def run(extract):
            sel = [jnp.zeros((PEER_NCAND, tb), f32)]

            def put(r, m, oh):
                sel[0] = jnp.where(oh, 1.0, sel[0])

            extract(cand, put)
            sel_scr[...] = sel[0]
            return sel[0]

        sel_fast = run(_extract_topk_distinct)

        @pl.when(_any_count_off(sel_fast))
        def _():
            run(lambda s, put: _extract_topk(s, cidx, 1e9, put))

        sel = sel_scr[...]
        m1 = v1[0:1, :]
        m2 = v2[0:1, :]
        wgt = sel * _dot_sel_lhs(p1.astype(bf16), jnp.exp(v1 - m1)) * _dot_sel_lhs(p2.astype(bf16), jnp.exp(v2 - m2))
        zsum = jnp.sum(wgt, axis=0, keepdims=True)
        cnt = _bdot(qsel_ref[...].astype(bf16), sel.astype(bf16))
        r1 = rank_scr[2 * h]
        r2 = rank_scr[2 * h + 1]
        jc = jnp.zeros((PEER_NKEYS, tb), f32)
        for i in range(PEER_TOPK):
            jc = jnp.where(r1 == float(i), cnt[i:i + 1, :], jc)
        r2_ref[h] = r2.astype(bf16)
        jc_ref[h] = jc
        e1_ref[h] = jnp.where(r1 < float(PEER_TOPK), jnp.exp(s_scr[2 * h] - m1), 0.0)
        e2_ref[h] = (jnp.where(r2 < float(PEER_TOPK), jnp.exp(s_scr[2 * h + 1] - m2), 0.0) / zsum).astype(bf16)
        return carry

    lax.fori_loop(0, PEER_HEADS, stage2, 0)


def _peer_topk(h, nw, wqt, sk, p1, p2, cidx, qsel):
    s = h.shape[0]
    tb = min(s, TB_TOPK)
    gate_spec = pl.BlockSpec((PEER_HEADS, PEER_NKEYS, tb), lambda i: (0, 0, i))
    gate_shape = jax.ShapeDtypeStruct((PEER_HEADS, PEER_NKEYS, s), f32)
    gate_shape16 = jax.ShapeDtypeStruct((PEER_HEADS, PEER_NKEYS, s), bf16)
    return pl.pallas_call(
        _topk_kernel, grid=(s // tb,),
        in_specs=[_rows(tb, D_MODEL), _full((1, D_MODEL)), _full((D_MODEL, D_MODEL)),
                  _full((2 * PEER_HEADS, PEER_NKEYS, PEER_HALF)), _full((PEER_NCAND, PEER_TOPK)),
                  _full((PEER_NCAND, PEER_TOPK)), _full((PEER_NCAND, 1)), _full((PEER_TOPK, PEER_NCAND))],
        out_specs=[pl.BlockSpec((D_MODEL, tb), lambda i: (0, i)), gate_spec, gate_spec, gate_spec, gate_spec],
        out_shape=[jax.ShapeDtypeStruct((D_MODEL, s), bf16), gate_shape16, gate_shape, gate_shape, gate_shape16],
        scratch_shapes=[pltpu.VMEM((2 * PEER_HEADS, PEER_NKEYS, tb), f32),
                        pltpu.VMEM((2 * PEER_HEADS, PEER_NKEYS, tb), f32),
                        pltpu.VMEM((2 * PEER_HEADS, PEER_TOPK, tb), f32), pltpu.VMEM((PEER_NCAND, tb), f32)],
        compiler_params=_params(), name="peer_topk")(h, nw, wqt, sk, p1, p2, cidx, qsel)


def _peer_kernel(zt_ref, h_ref, u_ref, vt_ref, r2_ref, jc_ref, e1_ref, e2_ref, out_ref, acc_ref, act_ref, hh_ref):
    j = pl.program_id(1)
    tb = h_ref.shape[0]
    ne = u_ref.shape[0]
    na = ne // PEER_NKEYS
    n_tiles = pl.num_programs(1) - 2

    @pl.when(j == 0)
    def _():
        acc_ref[...] = jnp.zeros_like(acc_ref)
        act_ref[...] = jnp.zeros_like(act_ref)
        hh_ref[...] = jnp.zeros_like(hh_ref)

    cur = j % 2
    prev = 1 - cur
    tile = jnp.clip(j - 1, 0, n_tiles - 1)
    n_mm = na
    rows_v = acc_ref.shape[0] // n_mm
    rows_u = ne // n_mm
    for ai in range(na):
        mi = ai * n_mm // na
        if ai * n_mm % na == 0:
            v0 = mi * rows_v
            acc_ref[v0:v0 + rows_v, :] += jnp.dot(vt_ref[v0:v0 + rows_v, :], hh_ref[cur],
                                                   preferred_element_type=f32)
        a = tile * na + ai
        g = jnp.zeros((PEER_NKEYS, tb), bf16)
        for h in range(PEER_HEADS):
            jc = jc_ref[h, pl.ds(a, 1), :].astype(bf16)
            e1 = e1_ref[h, pl.ds(a, 1), :].astype(bf16)
            g = g + jnp.where(r2_ref[h] < jc, e2_ref[h] * e1, jnp.zeros_like(g))
        r0 = ai * PEER_NKEYS
        hh_ref[prev, r0:r0 + PEER_NKEYS, :] = g * jax.nn.gelu(act_ref[prev, r0:r0 + PEER_NKEYS, :].astype(bf16))
        if (ai + 1) * n_mm % na == 0:
            u0 = mi * rows_u
            act_ref[cur, u0:u0 + rows_u, :] = jnp.dot(u_ref[u0:u0 + rows_u, :], zt_ref[...],
                                                      preferred_element_type=f32)

    @pl.when(j == pl.num_programs(1) - 1)
    def _():
        out_ref[...] = h_ref[...] + acc_ref[...].T


def _peer_dense(zt, h, u, vt, r2, jc, e1, e2):
    s = h.shape[0]
    tb = min(s, TB_PEER)
    n_tiles = u.shape[0] // NE_TILE
    gate_spec = pl.BlockSpec((PEER_HEADS, PEER_NKEYS, tb), lambda i, j: (0, 0, i))
    return pl.pallas_call(
        _peer_kernel, grid=(s // tb, n_tiles + 2),
        in_specs=[pl.BlockSpec((D_MODEL, tb), lambda i, j: (0, i)), pl.BlockSpec((tb, D_MODEL), lambda i, j: (i, 0)),
                  pl.BlockSpec((NE_TILE, D_MODEL), lambda i, j: (jnp.minimum(j, n_tiles - 1), 0)),
                  pl.BlockSpec((None, D_MODEL, NE_TILE), lambda i, j: (jnp.clip(j - 2, 0, n_tiles - 1), 0, 0)),
                  gate_spec, gate_spec, gate_spec, gate_spec],
        out_specs=pl.BlockSpec((tb, D_MODEL), lambda i, j: (i, 0)),
        out_shape=jax.ShapeDtypeStruct((s, D_MODEL), f32),
        scratch_shapes=[pltpu.VMEM((D_MODEL, tb), f32), pltpu.VMEM((2, NE_TILE, tb), f32),
                        pltpu.VMEM((2, NE_TILE, tb), bf16)],
        compiler_params=_params(2), name="peer_dense")(zt, h, u, vt, r2, jc, e1, e2)


def _ple_kernel(h_ref, p_ref, nw_ref, wg_ref, wp_ref, fw_ref, out_ref, *, final):
    x = h_ref[...]
    gate = jax.nn.sigmoid(jnp.dot(_rms(x, nw_ref[...]).astype(bf16), wg_ref[...], preferred_element_type=f32))
    y = x + jnp.dot(p_ref[...].astype(bf16), wp_ref[...], preferred_element_type=f32) * gate
    if final:
        y = _rms(y, fw_ref[...])
    out_ref[...] = y


def _ple(h, p, nw, wg, wp, fw, final):
    s = h.shape[0]
    tb = min(s, TB_PLE)
    pw = p.shape[1]
    return pl.pallas_call(
        functools.partial(_ple_kernel, final=final), grid=(s // tb,),
        in_specs=[_rows(tb, D_MODEL), _rows(tb, pw), _full((1, D_MODEL)), _full((D_MODEL, D_MODEL)),
                  _full((pw, D_MODEL)), _full((1, D_MODEL))],
        out_specs=_rows(tb, D_MODEL),
        out_shape=jax.ShapeDtypeStruct((s, D_MODEL), f32),
        compiler_params=_params(), name="ple_gate")(h, p, nw, wg, wp, fw)


def _block_diag(blocks):
    n, r, c = blocks.shape
    return jnp.einsum('nrc,nm->nrmc', blocks, jnp.eye(n, dtype=blocks.dtype)).reshape(n * r, n * c)


def _retention_consts():
    log_gamma = jnp.log(1.0 - 2.0 ** (-5.0 - jnp.arange(RET_HEADS, dtype=f32)))
    idx = jnp.arange(RET_CHUNK, dtype=f32)
    diff = idx[:, None] - idx[None, :]
    causal = diff >= 0
    dmask = jnp.where(causal, jnp.exp(log_gamma[:, None, None] * jnp.where(causal, diff, 0.0)), 0.0)
    q_decay = jnp.exp(log_gamma[:, None] * (idx + 1.0))
    k_decay = jnp.exp(log_gamma[:, None] * (RET_CHUNK - 1.0 - idx))
    chunk_decay = jnp.exp(log_gamma * RET_CHUNK)
    qdec = jnp.repeat(q_decay.T, RET_HD, axis=1)
    kdec = jnp.repeat(k_decay.T, RET_HD, axis=1)
    bmask = _block_diag(jnp.ones((RET_HEADS, RET_HD, RET_HD), f32))
    cdec = _block_diag(jnp.broadcast_to(chunk_decay[:, None, None], (RET_HEADS, RET_HD, RET_HD)))
    return dmask, qdec, kdec, cdec, bmask, bmask


def _s5_discretize(a_re, a_im, b_re, b_im, c_re, c_im, log_dt):
    dt = jnp.exp(log_dt)[:, None]
    mag = jnp.exp(a_re * dt)
    ang = a_im * dt
    ab_re, ab_im = mag * jnp.cos(ang), mag * jnp.sin(ang)
    den = a_re * a_re + a_im * a_im
    p_re, p_im = ab_re - 1.0, ab_im
    f_re = (p_re * a_re + p_im * a_im) / den
    f_im = (p_im * a_re - p_re * a_im) / den
    bb_re = f_re[..., None] * b_re - f_im[..., None] * b_im
    bb_im = f_re[..., None] * b_im + f_im[..., None] * b_re
    abar = jnp.stack([ab_re.reshape(-1), ab_im.reshape(-1)])
    bre = _block_diag(jnp.swapaxes(bb_re, 1, 2))
    bim = _block_diag(jnp.swapaxes(bb_im, 1, 2))
    cre = _block_diag(jnp.swapaxes(c_re, 1, 2))
    cim = _block_diag(jnp.swapaxes(c_im, 1, 2))
    return abar, bre, bim, cre, cim


def _peer_rank_consts():
    pairs = [(i, j) for i in range(PEER_TOPK) for j in range(PEER_TOPK) if (i + 1) * (j + 1) <= PEER_TOPK]
    p1 = np.zeros((PEER_NCAND, PEER_TOPK), np.float32)
    p2 = np.zeros((PEER_NCAND, PEER_TOPK), np.float32)
    cidx = np.full((PEER_NCAND, 1), 1e9, np.float32)
    for c, (i, j) in enumerate(pairs):
        p1[c, i] = 1.0
        p2[c, j] = 1.0
        cidx[c, 0] = PEER_TOPK * i + j
    return jnp.asarray(p1), jnp.asarray(p2), jnp.asarray(cidx), jnp.asarray(p1.T)


def _row(v):
    return v.reshape(1, -1).astype(f32)


def kernel(x, p, positions, mix_norm, w_in, ret_gn, lru_conv_w, lru_conv_b, lru_w_a, lru_b_a, lru_w_x, lru_b_x, lru_lambda, s5_a_re, s5_a_im, s5_b_re, s5_b_im, s5_c_re, s5_c_im, s5_d, s5_log_dt, s5_glu_w, s5_glu_b, gdn_conv_w, gdn_a_log, gdn_dt_bias, gdn_norm, branch_norm, w_out, ffn_norm, peer_wq, peer_subkeys, peer_u, peer_v, ple_norm, ple_wg, ple_wp, final_norm):
    bsz, seq, _ = x.shape
    depth = w_in.shape[0]
    assert bsz == 1
    h = x.reshape(seq, D_MODEL)

    half = RET_HD // 2
    inv_freq = ROPE_BASE ** (-jnp.arange(half, dtype=f32) / half)
    freq = jnp.tile(inv_freq, 128 // half).reshape(1, 128)
    cos, sin = _rope_tables(positions.reshape(seq, 1).astype(f32), freq)

    ret_consts = _retention_consts()
    bones = ret_consts[5]
    p1, p2, cidx, qsel = _peer_rank_consts()
    head_rows = jnp.arange(128)[:, None]
    head_lanes = jnp.arange(GROUP_W)[None, :] // GDN_HD
    eb = (head_rows == head_lanes).astype(f32)
    ea = (head_rows == head_lanes + GDN_HEADS).astype(f32)

    for l in range(depth):
        w = w_in[l]
        w_perm = jnp.concatenate([w[:, 0:2560], w[:, 2568:2824], w[:, 2560:2568],
                                  jnp.zeros((D_MODEL, IN_PAD - 2824), f32)], axis=1).astype(bf16)
        ret_in, lru_in, s5_in, gdn_in, ba_in = _in_proj(h, _row(mix_norm[l]), w_perm)

        y_ret = _retention(ret_in, cos, sin, ret_consts, _row(ret_gn[l]), _row(branch_norm[l, 0]))
        y_lru = _rglru(lru_in, lru_conv_w[l], _row(lru_conv_b[l]), _block_diag(lru_w_a[l]), _row(lru_b_a[l]),
                       _block_diag(lru_w_x[l]), _row(lru_b_x[l]), _row(lru_lambda[l]), _row(branch_norm[l, 1]))
        abar, bre, bim, cre, cim = _s5_discretize(s5_a_re[l], s5_a_im[l], s5_b_re[l], s5_b_im[l],
                                                  s5_c_re[l], s5_c_im[l], s5_log_dt[l])
        y_s5 = _s5(s5_in, abar, bre, bim, cre, cim, _row(s5_d[l]), s5_glu_w[l], _row(s5_glu_b[l]),
                   _row(branch_norm[l, 2]))
        alog = jnp.zeros((1, 128), f32).at[0, GDN_HEADS:2 * GDN_HEADS].set(gdn_a_log[l])
        dtb = jnp.zeros((1, 128), f32).at[0, GDN_HEADS:2 * GDN_HEADS].set(gdn_dt_bias[l])
        y_gdn = _gdn(gdn_in, ba_in, gdn_conv_w[l], alog, dtb, eb, ea, bones,
                     _row(jnp.tile(gdn_norm[l], GDN_HEADS)), _row(branch_norm[l, 3]))

        h = _out_proj(h, (y_ret, y_lru, y_s5, y_gdn), w_out[l].astype(bf16))

        sk = peer_subkeys[l].reshape(2 * PEER_HEADS, PEER_NKEYS, PEER_HALF)
        zt, r2, jc, e1, e2 = _peer_topk(h, _row(ffn_norm[l]), peer_wq[l].T.astype(bf16), sk, p1, p2, cidx, qsel)
        vt_tiles = peer_v[l].astype(bf16).reshape(-1, NE_TILE, D_MODEL).swapaxes(1, 2)
        h = _peer_dense(zt, h, peer_u[l].astype(bf16), vt_tiles, r2, jc, e1, e2)

        h = _ple(h, p[l].reshape(seq, -1), _row(ple_norm[l]), ple_wg[l].astype(bf16), ple_wp[l].astype(bf16),
                 _row(final_norm), final=(l == depth - 1))
    return h.reshape(bsz, seq, D_MODEL)
```

```python
import functools
import math

import jax
import jax.numpy as jnp
import numpy as np
from jax import lax
from jax.experimental import pallas as pl
from jax.experimental.pallas import tpu as pltpu

f32 = jnp.float32
bf16 = jnp.bfloat16
_HI = lax.Precision.HIGHEST
_NT = (((1,), (1,)), ((), ()))
_TN = (((0,), (0,)), ((), ()))

EPS = 1e-6
D_MODEL = 1024
GROUP_W = 256
CONV_K = 4
RET_HEADS = 4
RET_HD = 64
RET_CHUNK = 128
ROPE_BASE = 10000.0
LRU_C = 8.0
S5_GROUPS = 16
S5_GW = 16
S5_STATE = 64
S5_N = S5_GROUPS * S5_STATE
S5_TILE = 8
GDN_HEADS = 4
GDN_HD = 64
GDN_CHUNK = 64
PEER_HEADS = 8
PEER_NKEYS = 128
PEER_HALF = 64
PEER_TOPK = 16
PEER_NCAND = 64

IN_PAD = 2944
VMEM_LIMIT = 48 * 1024 * 1024

TB_IN = 256
TB_RET = 512
TB_LRU = 512
TB_S5 = 512
TB_GDN = 256
TB_OUT = 512
TB_TOPK = 256
TB_PEER = 512
NE_TILE = 512
TB_PLE = 512


def _dot(a, b, dims=(((1,), (0,)), ((), ())), hi=False):
    if hi:
        return lax.dot_general(a, b, dims, precision=_HI, preferred_element_type=f32)
    return lax.dot_general(a.astype(bf16), b.astype(bf16), dims, preferred_element_type=f32)


def _split(a):
    hi = a.astype(bf16)
    return hi, (a - hi.astype(f32)).astype(bf16)


def _bdot(a, b, dims=(((1,), (0,)), ((), ()))):
    return lax.dot_general(a, b, dims, preferred_element_type=f32)


def _dot3(a, b, dims=(((1,), (0,)), ((), ()))):
    (ah, al), (bh, bl) = a, b
    return _bdot(ah, bh, dims) + (_bdot(ah, bl, dims) + _bdot(al, bh, dims))


def _dot_sel(x, sel):
    xh, xl = _split(x)
    s = sel.astype(bf16)
    return _bdot(xh, s) + _bdot(xl, s)


def _dot_sel_lhs(sel, x):
    xh, xl = _split(x)
    return _bdot(sel, xh) + _bdot(sel, xl)


def _rms(x, w):
    return x * lax.rsqrt(jnp.mean(x * x, axis=-1, keepdims=True) + EPS) * w


def _shift_rows(x, d, fill):
    row = lax.broadcasted_iota(jnp.int32, x.shape, 0)
    return jnp.where(row >= d, pltpu.roll(x, d, 0), fill)


def _params(n_axes=1):
    return pltpu.CompilerParams(dimension_semantics=("arbitrary",) * n_axes, vmem_limit_bytes=VMEM_LIMIT)


def _full(shape):
    n = len(shape)
    return pl.BlockSpec(shape, lambda *_: (0,) * n)


def _rows(tb, w):
    return pl.BlockSpec((tb, w), lambda i: (i, 0))


def _rope_kernel(pos_ref, freq_ref, cos_ref, sin_ref):
    ang = pos_ref[...] * freq_ref[...]
    cos_ref[...] = jnp.cos(ang)
    sin_ref[...] = jnp.sin(ang)


def _rope_tables(pos, freq):
    s = pos.shape[0]
    tb = min(s, 1024)
    return pl.pallas_call(
        _rope_kernel, grid=(s // tb,),
        in_specs=[_rows(tb, 1), _full((1, 128))],
        out_specs=[_rows(tb, 128), _rows(tb, 128)],
        out_shape=[jax.ShapeDtypeStruct((s, 128), f32)] * 2,
        compiler_params=_params(), name="rope_tables")(pos, freq)


def _in_kernel(h_ref, nw_ref, w_ref, ret_ref, lru_ref, s5_ref, gdn_ref, ba_ref):
    z = _rms(h_ref[...], nw_ref[...]).astype(bf16)
    off = 0
    for ref in (ret_ref, lru_ref, s5_ref, gdn_ref, ba_ref):
        w = ref.shape[1]
        ref[...] = jnp.dot(z, w_ref[:, off:off + w], preferred_element_type=f32)
        off += w


def _in_proj(h, nw, w):
    s = h.shape[0]
    tb = min(s, TB_IN)
    widths = (1024, 512, 256, 1024, 128)
    return pl.pallas_call(
        _in_kernel, grid=(s // tb,),
        in_specs=[_rows(tb, D_MODEL), _full((1, D_MODEL)), _full((D_MODEL, IN_PAD))],
        out_specs=[_rows(tb, w_) for w_ in widths],
        out_shape=[jax.ShapeDtypeStruct((s, w_), f32) for w_ in widths],
        compiler_params=_params(), name="in_proj")(h, nw, w)


def _ret_kernel(x_ref, cos_ref, sin_ref, dmask_ref, qdec_ref, kdec_ref, cdec_ref, bmask_ref, bones_ref,
                gn_ref, bn_ref, out_ref, state_ref, o_scr):
    tb = x_ref.shape[0]

    @pl.when(pl.program_id(0) == 0)
    def _():
        state_ref[...] = jnp.zeros_like(state_ref)

    lane = lax.broadcasted_iota(jnp.int32, (1, GROUP_W), 1)
    first = (lane % RET_HD) < (RET_HD // 2)
    cosf = jnp.concatenate([cos_ref[...], cos_ref[...]], axis=1)
    sinf = jnp.concatenate([sin_ref[...], sin_ref[...]], axis=1) * jnp.where(first, -1.0, 1.0)

    def rot(t):
        partner = jnp.where(first, pltpu.roll(t, GROUP_W - RET_HD // 2, 1), pltpu.roll(t, RET_HD // 2, 1))
        return t * cosf + partner * sinf

    q = rot(x_ref[:, 0:256]) * (RET_HD ** -0.5)
    k = rot(x_ref[:, 256:512])
    v = x_ref[:, 512:768]
    qdec = qdec_ref[...]
    kdec = kdec_ref[...]
    for c in range(tb // RET_CHUNK):
        r0 = c * RET_CHUNK
        qc, kc, vc = q[r0:r0 + RET_CHUNK], k[r0:r0 + RET_CHUNK], v[r0:r0 + RET_CHUNK]
        st = state_ref[...]
        o = _dot(qc, st) * qdec
        for h in range(RET_HEADS):
            hm = (lane // RET_HD) == h
            sc = _dot(jnp.where(hm, qc, 0.0), kc, _NT) * dmask_ref[h]
            o = o + _dot(sc, jnp.where(hm, vc, 0.0))
        kv = _dot(kc * kdec, vc, _TN)
        state_ref[...] = st * cdec_ref[...] + kv * bmask_ref[...]
        o_scr[r0:r0 + RET_CHUNK, :] = o
    o = o_scr[...]
    bones = bones_ref[...]
    mu = _dot(o, bones, hi=True) * (1.0 / RET_HD)
    d = o - mu
    var = _dot(d * d, bones, hi=True) * (1.0 / RET_HD)
    on = d * lax.rsqrt(var + EPS) * gn_ref[...]
    g = x_ref[:, 768:1024]
    y = g * jax.nn.sigmoid(g) * on
    out_ref[...] = _rms(y, bn_ref[...])


def _retention(ret_in, cos, sin, consts, gn, bn):
    s = ret_in.shape[0]
    tb = min(s, TB_RET)
    dmask, qdec, kdec, cdec, bmask, bones = consts
    return pl.pallas_call(
        _ret_kernel, grid=(s // tb,),
        in_specs=[_rows(tb, 1024), _rows(tb, 128), _rows(tb, 128), _full(dmask.shape), _full(qdec.shape),
                  _full(kdec.shape), _full(cdec.shape), _full(bmask.shape), _full(bones.shape),
                  _full((1, GROUP_W)), _full((1, GROUP_W))],
        out_specs=_rows(tb, GROUP_W),
        out_shape=jax.ShapeDtypeStruct((s, GROUP_W), f32),
        scratch_shapes=[pltpu.VMEM((GROUP_W, GROUP_W), f32), pltpu.VMEM((tb, GROUP_W), f32)],
        compiler_params=_params(), name="retention")(ret_in, cos, sin, dmask, qdec, kdec, cdec, bmask, bones, gn, bn)


def _lru_kernel(x_ref, cw_ref, cb_ref, wa_ref, ba_ref, wx_ref, bx_ref, lam_ref, bn_ref, out_ref, xbuf, carry_ref,
                h_scr):
    tb = x_ref.shape[0]

    @pl.when(pl.program_id(0) == 0)
    def _():
        xbuf[0:8, :] = jnp.zeros((8, GROUP_W), f32)
        carry_ref[...] = jnp.zeros_like(carry_ref)

    x = x_ref[:, 256:512]
    xbuf[8:8 + tb, :] = x
    cw = cw_ref[...]
    xb = (cw[3:4] * x + cw[2:3] * xbuf[7:7 + tb, :] + cw[1:2] * xbuf[6:6 + tb, :]
          + cw[0:1] * xbuf[5:5 + tb, :]) + cb_ref[...]
    xbuf[0:8, :] = xbuf[tb:tb + 8, :]
    r = jax.nn.sigmoid(_dot(xb, wa_ref[...], hi=True) + ba_ref[...])
    i = jax.nn.sigmoid(_dot(xb, wx_ref[...], hi=True) + bx_ref[...])
    log_a = -LRU_C * r * jax.nn.softplus(-lam_ref[...])
    a = jnp.exp(log_a)
    y2 = 2.0 * log_a
    one_minus_a2 = -jnp.tanh(0.5 * y2) * (jnp.exp(y2) + 1.0)
    b = jnp.sqrt(one_minus_a2) * (i * xb)
    row8 = lax.broadcasted_iota(jnp.int32, b.shape, 0) % S5_TILE
    d = 1
    while d < S5_TILE:
        b = a * jnp.where(row8 >= d, pltpu.roll(b, d, 0), 0.0) + b
        a = a * jnp.where(row8 >= d, pltpu.roll(a, d, 0), 1.0)
        d *= 2
    c = carry_ref[...]
    for t in range(tb // S5_TILE):
        r0 = t * S5_TILE
        ht = b[r0:r0 + S5_TILE, :] + a[r0:r0 + S5_TILE, :] * c
        h_scr[r0:r0 + S5_TILE, :] = ht
        c = ht[S5_TILE - 1:S5_TILE, :]
    carry_ref[...] = c
    y = jax.nn.gelu(x_ref[:, 0:256]) * h_scr[...]
    out_ref[...] = _rms(y, bn_ref[...])


def _rglru(lru_in, cw, cb, wa, ba, wx, bx, lam, bn):
    s = lru_in.shape[0]
    tb = min(s, TB_LRU)
    row = _full((1, GROUP_W))
    sq = _full((GROUP_W, GROUP_W))
    return pl.pallas_call(
        _lru_kernel, grid=(s // tb,),
        in_specs=[_rows(tb, 512), _full((CONV_K, GROUP_W)), row, sq, row, sq, row, row, row],
        out_specs=_rows(tb, GROUP_W),
        out_shape=jax.ShapeDtypeStruct((s, GROUP_W), f32),
        scratch_shapes=[pltpu.VMEM((tb + 8, GROUP_W), f32), pltpu.VMEM((1, GROUP_W), f32),
                        pltpu.VMEM((tb, GROUP_W), f32)],
        compiler_params=_params(), name="rglru")(lru_in, cw, cb, wa, ba, wx, bx, lam, bn)


def _s5_kernel(u_ref, a_ref, bre_ref, bim_ref, cre_ref, cim_ref, d_ref, gw_ref, gb_ref, bn_ref, out_ref, carry_ref,
               xr_scr, xi_scr):
    tb = u_ref.shape[0]

    @pl.when(pl.program_id(0) == 0)
    def _():
        carry_ref[...] = jnp.zeros_like(carry_ref)

    u = u_ref[...]
    ar = a_ref[0:1, :]
    ai = a_ref[1:2, :]
    xr = _dot(u, bre_ref[...])
    xi = _dot(u, bim_ref[...])
    row8 = lax.broadcasted_iota(jnp.int32, xr.shape, 0) % S5_TILE
    prow = lax.broadcasted_iota(jnp.int32, (S5_TILE, S5_N), 0)
    pr = jnp.broadcast_to(ar, (S5_TILE, S5_N))
    pi = jnp.broadcast_to(ai, (S5_TILE, S5_N))
    d = 1
    while d < S5_TILE:
        sr = jnp.where(row8 >= d, pltpu.roll(xr, d, 0), 0.0)
        si = jnp.where(row8 >= d, pltpu.roll(xi, d, 0), 0.0)
        xr, xi = xr + (ar * sr - ai * si), xi + (ar * si + ai * sr)
        qr = jnp.where(prow >= d, pltpu.roll(pr, d, 0), 1.0)
        qi = jnp.where(prow >= d, pltpu.roll(pi, d, 0), 0.0)
        pr, pi = pr * qr - pi * qi, pr * qi + pi * qr
        ar, ai = ar * ar - ai * ai, 2.0 * ar * ai
        d *= 2
    cr = carry_ref[0:1, :]
    ci = carry_ref[1:2, :]
    for t in range(tb // S5_TILE):
        r0 = t * S5_TILE
        tr = xr[r0:r0 + S5_TILE, :] + (pr * cr - pi * ci)
        ti = xi[r0:r0 + S5_TILE, :] + (pr * ci + pi * cr)
        xr_scr[r0:r0 + S5_TILE, :] = tr
        xi_scr[r0:r0 + S5_TILE, :] = ti
        cr = tr[S5_TILE - 1:S5_TILE, :]
        ci = ti[S5_TILE - 1:S5_TILE, :]
    carry_ref[0:1, :] = cr
    carry_ref[1:2, :] = ci
    y = _dot(xr_scr[...], cre_ref[...]) - _dot(xi_scr[...], cim_ref[...]) + d_ref[...] * u
    zz = _dot(jax.nn.gelu(y), gw_ref[...]) + gb_ref[...]
    o = zz[:, 0:GROUP_W] * jax.nn.sigmoid(zz[:, GROUP_W:2 * GROUP_W])
    out_ref[...] = _rms(o, bn_ref[...])


def _s5(s5_in, a, bre, bim, cre, cim, dd, gw, gb, bn):
    s = s5_in.shape[0]
    tb = min(s, TB_S5)
    row = _full((1, GROUP_W))
    return pl.pallas_call(
        _s5_kernel, grid=(s // tb,),
        in_specs=[_rows(tb, GROUP_W), _full((2, S5_N)), _full((GROUP_W, S5_N)), _full((GROUP_W, S5_N)),
                  _full((S5_N, GROUP_W)), _full((S5_N, GROUP_W)), row, _full((GROUP_W, 2 * GROUP_W)),
                  _full((1, 2 * GROUP_W)), row],
        out_specs=_rows(tb, GROUP_W),
        out_shape=jax.ShapeDtypeStruct((s, GROUP_W), f32),
        scratch_shapes=[pltpu.VMEM((2, S5_N), f32), pltpu.VMEM((tb, S5_N), f32), pltpu.VMEM((tb, S5_N), f32)],
        compiler_params=_params(), name="s5")(s5_in, a, bre, bim, cre, cim, dd, gw, gb, bn)


def _gdn_kernel(x_ref, ba_ref, cw_ref, alog_ref, dtb_ref, eb_ref, ea_ref, bones_ref, nw_ref, bn_ref, out_ref,
                xbuf, state_ref, o_scr):
    tb = x_ref.shape[0]
    c_ = GDN_CHUNK

    @pl.when(pl.program_id(0) == 0)
    def _():
        xbuf[0:8, :] = jnp.zeros((8, 3 * GROUP_W), f32)
        state_ref[...] = jnp.zeros_like(state_ref)

    x = x_ref[:, 0:768]
    xbuf[8:8 + tb, :] = x
    cw = cw_ref[...]
    y = cw[3:4] * x + cw[2:3] * xbuf[7:7 + tb, :] + cw[1:2] * xbuf[6:6 + tb, :] + cw[0:1] * xbuf[5:5 + tb, :]
    xbuf[0:8, :] = xbuf[tb:tb + 8, :]
    y = y * jax.nn.sigmoid(y)
    bones = bones_ref[...]
    q = y[:, 0:256]
    k = y[:, 256:512]
    v = y[:, 512:768]
    q = q * lax.rsqrt(_dot_sel(q * q, bones) + EPS) * (GDN_HD ** -0.5)
    k = k * lax.rsqrt(_dot_sel(k * k, bones) + EPS)
    ba = ba_ref[...]
    beta_b = _dot_sel(jax.nn.sigmoid(ba), eb_ref[...])
    la = -jnp.exp(alog_ref[...]) * jax.nn.softplus(ba + dtb_ref[...])
    la_b = _dot_sel(la, ea_ref[...])

    ri = lax.broadcasted_iota(jnp.int32, (c_, c_), 0)
    ci = lax.broadcasted_iota(jnp.int32, (c_, c_), 1)
    incl = ri >= ci
    strict = ri > ci
    eye = ri == ci
    ltri = incl.astype(bf16)
    ri4 = lax.broadcasted_iota(jnp.int32, (c_, GROUP_W), 0)
    ci4 = lax.broadcasted_iota(jnp.int32, (c_, GROUP_W), 1) % GDN_HD
    lev_masks = []
    b = 1
    while b < c_:
        lev_masks.append(((ri // (2 * b)) == (ci // (2 * b))) & (((ri // b) % 2) == 1) & (((ci // b) % 2) == 0))
        b *= 2

    n_chunks = tb // c_
    items = [(c, h) for c in range(n_chunks) for h in range(GDN_HEADS)]
    per_chunk = []
    for c in range(n_chunks):
        r0 = c * c_
        la_c = la_b[r0:r0 + c_, :]
        gcum = _dot_sel_lhs(ltri, la_c)
        dmat = _dot_sel_lhs(ltri, jnp.where(ri4 > ci4, la_c, 0.0))
        lm = jnp.where(ri4 >= ci4, jnp.exp(dmat), 0.0)
        eg = jnp.exp(gcum)
        gl = gcum[c_ - 1:c_, :]
        qc, kc, vc, bc = q[r0:r0 + c_, :], k[r0:r0 + c_, :], v[r0:r0 + c_, :], beta_b[r0:r0 + c_, :]
        kb = kc * bc
        per_chunk.append(dict(lm=lm, k=kc, q=qc, kb=kb, vb=vc * bc, kbe=kb * eg, qd=qc * eg,
                              kd=kc * jnp.exp(gl - gcum), cd=jnp.exp(gl)))
    lmat_f, lmat, attn = {}, {}, {}
    for (c, h) in items:
        cs = slice(h * GDN_HD, (h + 1) * GDN_HD)
        pc = per_chunk[c]
        lm = pc['lm'][:, cs]
        kh = pc['k'][:, cs]
        lmat_f[c, h] = jnp.where(strict, _dot(pc['kb'][:, cs], kh, _NT) * lm, 0.0)
        lmat[c, h] = _split(lmat_f[c, h])
        attn[c, h] = jnp.where(incl, _dot(pc['q'][:, cs], kh, _NT) * lm, 0.0)
    tinv = {n: _split(jnp.where(eye, 1.0, 0.0) - jnp.where(lev_masks[0], lmat_f[n], 0.0)) for n in items}
    for m in lev_masks[1:]:
        xs = {n: _split(_dot3((jnp.where(m, lmat[n][0], jnp.zeros_like(lmat[n][0])),
                               jnp.where(m, lmat[n][1], jnp.zeros_like(lmat[n][1]))), tinv[n])) for n in items}
        tinv = {n: _split(tinv[n][0].astype(f32) + tinv[n][1].astype(f32) - _dot3(tinv[n], xs[n])) for n in items}
    u, w, mm, nn, pp, rr = {}, {}, {}, {}, {}, {}
    for (c, h) in items:
        cs = slice(h * GDN_HD, (h + 1) * GDN_HD)
        pc = per_chunk[c]
        u[c, h] = _dot3(tinv[c, h], _split(pc['vb'][:, cs]))
        w[c, h] = _dot3(tinv[c, h], _split(pc['kbe'][:, cs]))
    for (c, h) in items:
        cs = slice(h * GDN_HD, (h + 1) * GDN_HD)
        pc = per_chunk[c]
        kd = _split(pc['kd'][:, cs])
        mm[c, h] = _split(jnp.where(eye, pc['cd'][:, cs], 0.0) - _dot3(kd, _split(w[c, h]), _TN))
        nn[c, h] = _dot3(kd, _split(u[c, h]), _TN)
        pp[c, h] = _dot(attn[c, h], u[c, h])
        rr[c, h] = _split(pc['qd'][:, cs] - _dot(attn[c, h], w[c, h]))
    for c in range(n_chunks):
        r0 = c * c_
        for h in range(GDN_HEADS):
            cs = slice(h * GDN_HD, (h + 1) * GDN_HD)
            st = _split(state_ref[h])
            o_scr[r0:r0 + c_, cs] = pp[c, h] + _dot3(rr[c, h], st)
            state_ref[h] = _dot3(mm[c, h], st) + nn[c, h]
    o = o_scr[...]
    on = o * lax.rsqrt(_dot_sel(o * o, bones) * (1.0 / GDN_HD) + EPS) * nw_ref[...]
    g = x_ref[:, 768:1024]
    yy = on * (g * jax.nn.sigmoid(g))
    out_ref[...] = _rms(yy, bn_ref[...])


def _gdn(gdn_in, ba, cw, alog, dtb, eb, ea, bones, nw, bn):
    s = gdn_in.shape[0]
    tb = min(s, TB_GDN)
    row = _full((1, GROUP_W))
    return pl.pallas_call(
        _gdn_kernel, grid=(s // tb,),
        in_specs=[_rows(tb, 1024), _rows(tb, 128), _full((CONV_K, 3 * GROUP_W)), _full((1, 128)), _full((1, 128)),
                  _full((128, GROUP_W)), _full((128, GROUP_W)), _full((GROUP_W, GROUP_W)), row, row],
        out_specs=_rows(tb, GROUP_W),
        out_shape=jax.ShapeDtypeStruct((s, GROUP_W), f32),
        scratch_shapes=[pltpu.VMEM((tb + 8, 3 * GROUP_W), f32), pltpu.VMEM((GDN_HEADS, GDN_HD, GDN_HD), f32),
                        pltpu.VMEM((tb, GROUP_W), f32)],
        compiler_params=_params(), name="gated_deltanet")(gdn_in, ba, cw, alog, dtb, eb, ea, bones, nw, bn)


def _out_kernel(h_ref, m0_ref, m1_ref, m2_ref, m3_ref, w_ref, out_ref):
    acc = h_ref[...]
    for j, m in enumerate((m0_ref, m1_ref, m2_ref, m3_ref)):
        acc = acc + jnp.dot(m[...].astype(bf16), w_ref[j * GROUP_W:(j + 1) * GROUP_W, :], preferred_element_type=f32)
    out_ref[...] = acc


def _out_proj(h, mixed, w):
    s = h.shape[0]
    tb = min(s, TB_OUT)
    return pl.pallas_call(
        _out_kernel, grid=(s // tb,),
        in_specs=[_rows(tb, D_MODEL)] + [_rows(tb, GROUP_W)] * 4 + [_full((D_MODEL, D_MODEL))],
        out_specs=_rows(tb, D_MODEL),
        out_shape=jax.ShapeDtypeStruct((s, D_MODEL), f32),
        compiler_params=_params(), name="out_proj")(h, *mixed, w)


def _extract_topk(s, idx_col, n_bad, put):
    for r in range(PEER_TOPK):
        m = jnp.max(s, axis=0, keepdims=True)
        idx = jnp.min(jnp.where(s == m, idx_col, n_bad), axis=0, keepdims=True)
        oh = idx_col == idx
        put(r, m, oh)
        s = jnp.where(oh, -jnp.inf, s)


def _extract_topk_distinct(s, put):
    for r in range(PEER_TOPK):
        m = jnp.max(s, axis=0, keepdims=True)
        oh = s == m
        put(r, m, oh)
        s = jnp.where(oh, -jnp.inf, s)


def _any_count_off(marks):
    n = jnp.sum(marks, axis=0, keepdims=True)
    return jnp.max(jnp.abs(n - float(PEER_TOPK))) > 0.5


def _topk_kernel(h_ref, nw_ref, wqt_ref, sk_ref, p1_ref, p2_ref, cidx_ref, qsel_ref,
                 zt_ref, r2_ref, jc_ref, e1_ref, e2_ref, s_scr, rank_scr, val_scr, sel_scr):
    tb = h_ref.shape[0]
    z = _rms(h_ref[...], nw_ref[...])
    zt = z.T.astype(bf16)
    zt_ref[...] = zt
    qt = jnp.dot(wqt_ref[...], zt, preferred_element_type=f32)
    for hp in range(2 * PEER_HEADS):
        s_scr[hp] = _dot(sk_ref[hp], qt[hp * PEER_HALF:(hp + 1) * PEER_HALF, :])

    key_idx = lax.broadcasted_iota(jnp.int32, (PEER_NKEYS, tb), 0).astype(f32)

    def stage1(h, carry):
        hps = (2 * h, 2 * h + 1)
        ranks = [jnp.full((PEER_NKEYS, tb), float(PEER_TOPK), f32) for _ in hps]
        ss = [s_scr[hp] for hp in hps]
        for r in range(PEER_TOPK):
            ms = [jnp.max(s, axis=0, keepdims=True) for s in ss]
            ohs = [s == m for s, m in zip(ss, ms)]
            for i, hp in enumerate(hps):
                val_scr[hp, r:r + 1, :] = ms[i]
                ranks[i] = jnp.where(ohs[i], float(r), ranks[i])
            ss = [jnp.where(oh, -jnp.inf, s) for s, oh in zip(ss, ohs)]
        for i, hp in enumerate(hps):
            rank_scr[hp] = ranks[i]
        marks = jnp.concatenate([jnp.where(rk < float(PEER_TOPK), 1.0, 0.0) for rk in ranks], axis=1)

        @pl.when(_any_count_off(marks))
        def _():
            for hp in hps:
                rank = [jnp.full((PEER_NKEYS, tb), float(PEER_TOPK), f32)]

                def put(r, m, oh, hp=hp, rank=rank):
                    val_scr[hp, r:r + 1, :] = m
                    rank[0] = jnp.where(oh, float(r), rank[0])

                _extract_topk(s_scr[hp], key_idx, float(PEER_NKEYS), put)
                rank_scr[hp] = rank[0]

        return carry

    lax.fori_loop(0, PEER_HEADS, stage1, 0)

    cidx = jnp.broadcast_to(cidx_ref[...], (PEER_NCAND, tb))
    p1 = p1_ref[...]
    p2 = p2_ref[...]

    def stage2(h, carry):
        v1 = val_scr[2 * h]
        v2 = val_scr[2 * h + 1]
        cand = _dot(p1, v1, hi=True) + _dot(p2, v2, hi=True)
        cand = jnp.where(cidx < 256.0, cand, -jnp.inf)
        def run(extract):
            sel = [jnp.zeros((PEER_NCAND, tb), f32)]

            def put(r, m, oh):
                sel[0] = jnp.where(oh, 1.0, sel[0])

            extract(cand, put)
            sel_scr[...] = sel[0]
            return sel[0]

        sel_fast = run(_extract_topk_distinct)

        @pl.when(_any_count_off(sel_fast))
        def _():
            run(lambda s, put: _extract_topk(s, cidx, 1e9, put))

        sel = sel_scr[...]
        m1 = v1[0:1, :]
        m2 = v2[0:1, :]
        wgt = sel * _dot_sel_lhs(p1.astype(bf16), jnp.exp(v1 - m1)) * _dot_sel_lhs(p2.astype(bf16), jnp.exp(v2 - m2))
        zsum = jnp.sum(wgt, axis=0, keepdims=True)
        cnt = _bdot(qsel_ref[...].astype(bf16), sel.astype(bf16))
        r1 = rank_scr[2 * h]
        r2 = rank_scr[2 * h + 1]
        jc = jnp.zeros((PEER_NKEYS, tb), f32)
        for i in range(PEER_TOPK):
            jc = jnp.where(r1 == float(i), cnt[i:i + 1, :], jc)
        r2_ref[h] = r2.astype(bf16)
        jc_ref[h] = jc
        e1_ref[h] = jnp.where(r1 < float(PEER_TOPK), jnp.exp(s_scr[2 * h] - m1), 0.0)
        e2_ref[h] = (jnp.where(r2 < float(PEER_TOPK), jnp.exp(s_scr[2 * h + 1] - m2), 0.0) / zsum).astype(bf16)
        return carry

    lax.fori_loop(0, PEER_HEADS, stage2, 0)


def _peer_topk(h, nw, wqt, sk, p1, p2, cidx, qsel):
    s = h.shape[0]
    tb = min(s, TB_TOPK)
    gate_spec = pl.BlockSpec((PEER_HEADS, PEER_NKEYS, tb), lambda i: (0, 0, i))
    gate_shape = jax.ShapeDtypeStruct((PEER_HEADS, PEER_NKEYS, s), f32)
    gate_shape16 = jax.ShapeDtypeStruct((PEER_HEADS, PEER_NKEYS, s), bf16)
    return pl.pallas_call(
        _topk_kernel, grid=(s // tb,),
        in_specs=[_rows(tb, D_MODEL), _full((1, D_MODEL)), _full((D_MODEL, D_MODEL)),
                  _full((2 * PEER_HEADS, PEER_NKEYS, PEER_HALF)), _full((PEER_NCAND, PEER_TOPK)),
                  _full((PEER_NCAND, PEER_TOPK)), _full((PEER_NCAND, 1)), _full((PEER_TOPK, PEER_NCAND))],
        out_specs=[pl.BlockSpec((D_MODEL, tb), lambda i: (0, i)), gate_spec, gate_spec, gate_spec, gate_spec],
        out_shape=[jax.ShapeDtypeStruct((D_MODEL, s), bf16), gate_shape16, gate_shape, gate_shape, gate_shape16],
        scratch_shapes=[pltpu.VMEM((2 * PEER_HEADS, PEER_NKEYS, tb), f32),
                        pltpu.VMEM((2 * PEER_HEADS, PEER_NKEYS, tb), f32),
                        pltpu.VMEM((2 * PEER_HEADS, PEER_TOPK, tb), f32), pltpu.VMEM((PEER_NCAND, tb), f32)],
        compiler_params=_params(), name="peer_topk")(h, nw, wqt, sk, p1, p2, cidx, qsel)


def _peer_kernel(zt_ref, h_ref, u_ref, vt_ref, r2_ref, jc_ref, e1_ref, e2_ref, out_ref, acc_ref, act_ref, hh_ref):
    j = pl.program_id(1)
    tb = h_ref.shape[0]
    ne = u_ref.shape[0]
    na = ne // PEER_NKEYS
    n_tiles = pl.num_programs(1) - 2

    @pl.when(j == 0)
    def _():
        acc_ref[...] = jnp.zeros_like(acc_ref)
        act_ref[...] = jnp.zeros_like(act_ref)
        hh_ref[...] = jnp.zeros_like(hh_ref)

    cur = j % 2
    prev = 1 - cur
    tile = jnp.clip(j - 1, 0, n_tiles - 1)
    n_mm = na
    rows_v = acc_ref.shape[0] // n_mm
    rows_u = ne // n_mm
    for ai in range(na):
        mi = ai * n_mm // na
        if ai * n_mm % na == 0:
            v0 = mi * rows_v
            acc_ref[v0:v0 + rows_v, :] += jnp.dot(vt_ref[v0:v0 + rows_v, :], hh_ref[cur],
                                                   preferred_element_type=f32)
        a = tile * na + ai
        g = jnp.zeros((PEER_NKEYS, tb), bf16)
        for h in range(PEER_HEADS):
            jc = jc_ref[h, pl.ds(a, 1), :].astype(bf16)
            e1 = e1_ref[h, pl.ds(a, 1), :].astype(bf16)
            g = g + jnp.where(r2_ref[h] < jc, e2_ref[h] * e1, jnp.zeros_like(g))
        r0 = ai * PEER_NKEYS
        hh_ref[prev, r0:r0 + PEER_NKEYS, :] = g * jax.nn.gelu(act_ref[prev, r0:r0 + PEER_NKEYS, :].astype(bf16))
        if (ai + 1) * n_mm % na == 0:
            u0 = mi * rows_u
            act_ref[cur, u0:u0 + rows_u, :] = jnp.dot(u_ref[u0:u0 + rows_u, :], zt_ref[...],
                                                      preferred_element_type=f32)

    @pl.when(j == pl.num_programs(1) - 1)
    def _():
        out_ref[...] = h_ref[...] + acc_ref[...].T


def _peer_dense(zt, h, u, vt, r2, jc, e1, e2):
    s = h.shape[0]
    tb = min(s, TB_PEER)
    n_tiles = u.shape[0] // NE_TILE
    gate_spec = pl.BlockSpec((PEER_HEADS, PEER_NKEYS, tb), lambda i, j: (0, 0, i))
    return pl.pallas_call(
        _peer_kernel, grid=(s // tb, n_tiles + 2),
        in_specs=[pl.BlockSpec((D_MODEL, tb), lambda i, j: (0, i)), pl.BlockSpec((tb, D_MODEL), lambda i, j: (i, 0)),
                  pl.BlockSpec((NE_TILE, D_MODEL), lambda i, j: (jnp.minimum(j, n_tiles - 1), 0)),
                  pl.BlockSpec((None, D_MODEL, NE_TILE), lambda i, j: (jnp.clip(j - 2, 0, n_tiles - 1), 0, 0)),
                  gate_spec, gate_spec, gate_spec, gate_spec],
        out_specs=pl.BlockSpec((tb, D_MODEL), lambda i, j: (i, 0)),
        out_shape=jax.ShapeDtypeStruct((s, D_MODEL), f32),
        scratch_shapes=[pltpu.VMEM((D_MODEL, tb), f32), pltpu.VMEM((2, NE_TILE, tb), f32),
                        pltpu.VMEM((2, NE_TILE, tb), bf16)],
        compiler_params=_params(2), name="peer_dense")(zt, h, u, vt, r2, jc, e1, e2)


def _ple_kernel(h_ref, p_ref, nw_ref, wg_ref, wp_ref, fw_ref, out_ref, *, final):
    x = h_ref[...]
    gate = jax.nn.sigmoid(jnp.dot(_rms(x, nw_ref[...]).astype(bf16), wg_ref[...], preferred_element_type=f32))
    y = x + jnp.dot(p_ref[...].astype(bf16), wp_ref[...], preferred_element_type=f32) * gate
    if final:
        y = _rms(y, fw_ref[...])
    out_ref[...] = y


def _ple_in_kernel(h_ref, p_ref, nw_ref, wg_ref, wp_ref, mw_ref, w_ref, h_out_ref, ret_ref, lru_ref, s5_ref, gdn_ref,
                   ba_ref):
    x = h_ref[...]
    gate = jax.nn.sigmoid(jnp.dot(_rms(x, nw_ref[...]).astype(bf16), wg_ref[...], preferred_element_type=f32))
    y = x + jnp.dot(p_ref[...].astype(bf16), wp_ref[...], preferred_element_type=f32) * gate
    h_out_ref[...] = y
    z = _rms(y, mw_ref[...]).astype(bf16)
    off = 0
    for ref in (ret_ref, lru_ref, s5_ref, gdn_ref, ba_ref):
        w = ref.shape[1]
        ref[...] = jnp.dot(z, w_ref[:, off:off + w], preferred_element_type=f32)
        off += w


def _ple_in(h, p, nw, wg, wp, mw, w):
    s = h.shape[0]
    tb = min(s, TB_IN)
    pw = p.shape[1]
    widths = (D_MODEL, 1024, 512, 256, 1024, 128)
    return pl.pallas_call(
        _ple_in_kernel, grid=(s // tb,),
        in_specs=[_rows(tb, D_MODEL), _rows(tb, pw), _full((1, D_MODEL)), _full((D_MODEL, D_MODEL)),
                  _full((pw, D_MODEL)), _full((1, D_MODEL)), _full((D_MODEL, IN_PAD))],
        out_specs=[_rows(tb, w_) for w_ in widths],
        out_shape=[jax.ShapeDtypeStruct((s, w_), f32) for w_ in widths],
        compiler_params=_params(), name="ple_in_proj")(h, p, nw, wg, wp, mw, w)


def _ple(h, p, nw, wg, wp, fw, final):
    s = h.shape[0]
    tb = min(s, TB_PLE)
    pw = p.shape[1]
    return pl.pallas_call(
        functools.partial(_ple_kernel, final=final), grid=(s // tb,),
        in_specs=[_rows(tb, D_MODEL), _rows(tb, pw), _full((1, D_MODEL)), _full((D_MODEL, D_MODEL)),
                  _full((pw, D_MODEL)), _full((1, D_MODEL))],
        out_specs=_rows(tb, D_MODEL),
        out_shape=jax.ShapeDtypeStruct((s, D_MODEL), f32),
        compiler_params=_params(), name="ple_gate")(h, p, nw, wg, wp, fw)


def _block_diag(blocks):
    n, r, c = blocks.shape
    return jnp.einsum('nrc,nm->nrmc', blocks, jnp.eye(n, dtype=blocks.dtype)).reshape(n * r, n * c)


def _retention_consts():
    log_gamma = jnp.log(1.0 - 2.0 ** (-5.0 - jnp.arange(RET_HEADS, dtype=f32)))
    idx = jnp.arange(RET_CHUNK, dtype=f32)
    diff = idx[:, None] - idx[None, :]
    causal = diff >= 0
    dmask = jnp.where(causal, jnp.exp(log_gamma[:, None, None] * jnp.where(causal, diff, 0.0)), 0.0)
    q_decay = jnp.exp(log_gamma[:, None] * (idx + 1.0))
    k_decay = jnp.exp(log_gamma[:, None] * (RET_CHUNK - 1.0 - idx))
    chunk_decay = jnp.exp(log_gamma * RET_CHUNK)
    qdec = jnp.repeat(q_decay.T, RET_HD, axis=1)
    kdec = jnp.repeat(k_decay.T, RET_HD, axis=1)
    bmask = _block_diag(jnp.ones((RET_HEADS, RET_HD, RET_HD), f32))
    cdec = _block_diag(jnp.broadcast_to(chunk_decay[:, None, None], (RET_HEADS, RET_HD, RET_HD)))
    return dmask, qdec, kdec, cdec, bmask, bmask


def _s5_discretize(a_re, a_im, b_re, b_im, c_re, c_im, log_dt):
    dt = jnp.exp(log_dt)[:, None]
    mag = jnp.exp(a_re * dt)
    ang = a_im * dt
    ab_re, ab_im = mag * jnp.cos(ang), mag * jnp.sin(ang)
    den = a_re * a_re + a_im * a_im
    p_re, p_im = ab_re - 1.0, ab_im
    f_re = (p_re * a_re + p_im * a_im) / den
    f_im = (p_im * a_re - p_re * a_im) / den
    bb_re = f_re[..., None] * b_re - f_im[..., None] * b_im
    bb_im = f_re[..., None] * b_im + f_im[..., None] * b_re
    abar = jnp.stack([ab_re.reshape(-1), ab_im.reshape(-1)])
    bre = _block_diag(jnp.swapaxes(bb_re, 1, 2))
    bim = _block_diag(jnp.swapaxes(bb_im, 1, 2))
    cre = _block_diag(jnp.swapaxes(c_re, 1, 2))
    cim = _block_diag(jnp.swapaxes(c_im, 1, 2))
    return abar, bre, bim, cre, cim


def _peer_rank_consts():
    pairs = [(i, j) for i in range(PEER_TOPK) for j in range(PEER_TOPK) if (i + 1) * (j + 1) <= PEER_TOPK]
    p1 = np.zeros((PEER_NCAND, PEER_TOPK), np.float32)
    p2 = np.zeros((PEER_NCAND, PEER_TOPK), np.float32)
    cidx = np.full((PEER_NCAND, 1), 1e9, np.float32)
    for c, (i, j) in enumerate(pairs):
        p1[c, i] = 1.0
        p2[c, j] = 1.0
        cidx[c, 0] = PEER_TOPK * i + j
    return jnp.asarray(p1), jnp.asarray(p2), jnp.asarray(cidx), jnp.asarray(p1.T)


def _row(v):
    return v.reshape(1, -1).astype(f32)


def kernel(x, p, positions, mix_norm, w_in, ret_gn, lru_conv_w, lru_conv_b, lru_w_a, lru_b_a, lru_w_x, lru_b_x, lru_lambda, s5_a_re, s5_a_im, s5_b_re, s5_b_im, s5_c_re, s5_c_im, s5_d, s5_log_dt, s5_glu_w, s5_glu_b, gdn_conv_w, gdn_a_log, gdn_dt_bias, gdn_norm, branch_norm, w_out, ffn_norm, peer_wq, peer_subkeys, peer_u, peer_v, ple_norm, ple_wg, ple_wp, final_norm):
    bsz, seq, _ = x.shape
    depth = w_in.shape[0]
    assert bsz == 1
    h = x.reshape(seq, D_MODEL)

    half = RET_HD // 2
    inv_freq = ROPE_BASE ** (-jnp.arange(half, dtype=f32) / half)
    freq = jnp.tile(inv_freq, 128 // half).reshape(1, 128)
    cos, sin = _rope_tables(positions.reshape(seq, 1).astype(f32), freq)

    ret_consts = _retention_consts()
    bones = ret_consts[5]
    p1, p2, cidx, qsel = _peer_rank_consts()
    head_rows = jnp.arange(128)[:, None]
    head_lanes = jnp.arange(GROUP_W)[None, :] // GDN_HD
    eb = (head_rows == head_lanes).astype(f32)
    ea = (head_rows == head_lanes + GDN_HEADS).astype(f32)

    def perm_w(l):
        w = w_in[l]
        return jnp.concatenate([w[:, 0:2560], w[:, 2568:2824], w[:, 2560:2568],
                                jnp.zeros((D_MODEL, IN_PAD - 2824), f32)], axis=1).astype(bf16)

    projected = _in_proj(h, _row(mix_norm[0]), perm_w(0))
    for l in range(depth):
        ret_in, lru_in, s5_in, gdn_in, ba_in = projected

        y_ret = _retention(ret_in, cos, sin, ret_consts, _row(ret_gn[l]), _row(branch_norm[l, 0]))
        y_lru = _rglru(lru_in, lru_conv_w[l], _row(lru_conv_b[l]), _block_diag(lru_w_a[l]), _row(lru_b_a[l]),
                       _block_diag(lru_w_x[l]), _row(lru_b_x[l]), _row(lru_lambda[l]), _row(branch_norm[l, 1]))
        abar, bre, bim, cre, cim = _s5_discretize(s5_a_re[l], s5_a_im[l], s5_b_re[l], s5_b_im[l],
                                                  s5_c_re[l], s5_c_im[l], s5_log_dt[l])
        y_s5 = _s5(s5_in, abar, bre, bim, cre, cim, _row(s5_d[l]), s5_glu_w[l], _row(s5_glu_b[l]),
                   _row(branch_norm[l, 2]))
        alog = jnp.zeros((1, 128), f32).at[0, GDN_HEADS:2 * GDN_HEADS].set(gdn_a_log[l])
        dtb = jnp.zeros((1, 128), f32).at[0, GDN_HEADS:2 * GDN_HEADS].set(gdn_dt_bias[l])
        y_gdn = _gdn(gdn_in, ba_in, gdn_conv_w[l], alog, dtb, eb, ea, bones,
                     _row(jnp.tile(gdn_norm[l], GDN_HEADS)), _row(branch_norm[l, 3]))

        h = _out_proj(h, (y_ret, y_lru, y_s5, y_gdn), w_out[l].astype(bf16))

        sk = peer_subkeys[l].reshape(2 * PEER_HEADS, PEER_NKEYS, PEER_HALF)
        zt, r2, jc, e1, e2 = _peer_topk(h, _row(ffn_norm[l]), peer_wq[l].T.astype(bf16), sk, p1, p2, cidx, qsel)
        vt_tiles = peer_v[l].astype(bf16).reshape(-1, NE_TILE, D_MODEL).swapaxes(1, 2)
        h = _peer_dense(zt, h, peer_u[l].astype(bf16), vt_tiles, r2, jc, e1, e2)

        if l == depth - 1:
            h = _ple(h, p[l].reshape(seq, -1), _row(ple_norm[l]), ple_wg[l].astype(bf16), ple_wp[l].astype(bf16),
                     _row(final_norm), final=True)
        else:
            h, *projected = _ple_in(h, p[l].reshape(seq, -1), _row(ple_norm[l]), ple_wg[l].astype(bf16),
                                    ple_wp[l].astype(bf16), _row(mix_norm[l + 1]), perm_w(l + 1))
    return h.reshape(bsz, seq, D_MODEL)
```

```python
import functools
import math

import jax
import jax.numpy as jnp
import numpy as np
from jax import lax
from jax.experimental import pallas as pl
from jax.experimental.pallas import tpu as pltpu

f32 = jnp.float32
bf16 = jnp.bfloat16
_HI = lax.Precision.HIGHEST
_NT = (((1,), (1,)), ((), ()))
_TN = (((0,), (0,)), ((), ()))

EPS = 1e-6
D_MODEL = 1024
GROUP_W = 256
CONV_K = 4
RET_HEADS = 4
RET_HD = 64
RET_CHUNK = 128
ROPE_BASE = 10000.0
LRU_C = 8.0
S5_GROUPS = 16
S5_GW = 16
S5_STATE = 64
S5_N = S5_GROUPS * S5_STATE
S5_TILE = 8
GDN_HEADS = 4
GDN_HD = 64
GDN_CHUNK = 64
PEER_HEADS = 8
PEER_NKEYS = 128
PEER_HALF = 64
PEER_TOPK = 16
PEER_NCAND = 64

IN_PAD = 2944
VMEM_LIMIT = 48 * 1024 * 1024

TB_IN = 256
TB_RET = 512
TB_LRU = 512
TB_S5 = 512
TB_GDN = 256
TB_OUT = 512
TB_TOPK = 256
TB_PEER = 512
NE_TILE = 512
TB_PLE = 512


def _dot(a, b, dims=(((1,), (0,)), ((), ())), hi=False):
    if hi:
        return lax.dot_general(a, b, dims, precision=_HI, preferred_element_type=f32)
    return lax.dot_general(a.astype(bf16), b.astype(bf16), dims, preferred_element_type=f32)


def _split(a):
    hi = a.astype(bf16)
    return hi, (a - hi.astype(f32)).astype(bf16)


def _bdot(a, b, dims=(((1,), (0,)), ((), ()))):
    return lax.dot_general(a, b, dims, preferred_element_type=f32)


def _dot3(a, b, dims=(((1,), (0,)), ((), ()))):
    (ah, al), (bh, bl) = a, b
    return _bdot(ah, bh, dims) + (_bdot(ah, bl, dims) + _bdot(al, bh, dims))


def _dot_sel(x, sel):
    xh, xl = _split(x)
    s = sel.astype(bf16)
    return _bdot(xh, s) + _bdot(xl, s)


def _dot_sel_lhs(sel, x):
    xh, xl = _split(x)
    return _bdot(sel, xh) + _bdot(sel, xl)


def _rms(x, w):
    return x * lax.rsqrt(jnp.mean(x * x, axis=-1, keepdims=True) + EPS) * w


def _shift_rows(x, d, fill):
    row = lax.broadcasted_iota(jnp.int32, x.shape, 0)
    return jnp.where(row >= d, pltpu.roll(x, d, 0), fill)


def _params(n_axes=1):
    return pltpu.CompilerParams(dimension_semantics=("arbitrary",) * n_axes, vmem_limit_bytes=VMEM_LIMIT)


def _full(shape):
    n = len(shape)
    return pl.BlockSpec(shape, lambda *_: (0,) * n)


def _rows(tb, w):
    return pl.BlockSpec((tb, w), lambda i: (i, 0))


def _rope_kernel(pos_ref, freq_ref, cos_ref, sin_ref):
    ang = pos_ref[...] * freq_ref[...]
    cos_ref[...] = jnp.cos(ang)
    sin_ref[...] = jnp.sin(ang)


def _rope_tables(pos, freq):
    s = pos.shape[0]
    tb = min(s, 1024)
    return pl.pallas_call(
        _rope_kernel, grid=(s // tb,),
        in_specs=[_rows(tb, 1), _full((1, 128))],
        out_specs=[_rows(tb, 128), _rows(tb, 128)],
        out_shape=[jax.ShapeDtypeStruct((s, 128), f32)] * 2,
        compiler_params=_params(), name="rope_tables")(pos, freq)


def _in_kernel(h_ref, nw_ref, w_ref, ret_ref, lru_ref, s5_ref, gdn_ref, ba_ref):
    z = _rms(h_ref[...], nw_ref[...]).astype(bf16)
    off = 0
    for ref in (ret_ref, lru_ref, s5_ref, gdn_ref, ba_ref):
        w = ref.shape[1]
        ref[...] = jnp.dot(z, w_ref[:, off:off + w], preferred_element_type=f32)
        off += w


def _in_proj(h, nw, w):
    s = h.shape[0]
    tb = min(s, TB_IN)
    widths = (1024, 512, 256, 1024, 128)
    return pl.pallas_call(
        _in_kernel, grid=(s // tb,),
        in_specs=[_rows(tb, D_MODEL), _full((1, D_MODEL)), _full((D_MODEL, IN_PAD))],
        out_specs=[_rows(tb, w_) for w_ in widths],
        out_shape=[jax.ShapeDtypeStruct((s, w_), f32) for w_ in widths],
        compiler_params=_params(), name="in_proj")(h, nw, w)


def _ret_kernel(x_ref, cos_ref, sin_ref, dmask_ref, qdec_ref, kdec_ref, cdec_ref, bmask_ref, bones_ref,
                gn_ref, bn_ref, out_ref, state_ref, o_scr):
    tb = x_ref.shape[0]

    @pl.when(pl.program_id(0) == 0)
    def _():
        state_ref[...] = jnp.zeros_like(state_ref)

    lane = lax.broadcasted_iota(jnp.int32, (1, GROUP_W), 1)
    first = (lane % RET_HD) < (RET_HD // 2)
    cosf = jnp.concatenate([cos_ref[...], cos_ref[...]], axis=1)
    sinf = jnp.concatenate([sin_ref[...], sin_ref[...]], axis=1) * jnp.where(first, -1.0, 1.0)

    def rot(t):
        partner = jnp.where(first, pltpu.roll(t, GROUP_W - RET_HD // 2, 1), pltpu.roll(t, RET_HD // 2, 1))
        return t * cosf + partner * sinf

    q = rot(x_ref[:, 0:256]) * (RET_HD ** -0.5)
    k = rot(x_ref[:, 256:512])
    v = x_ref[:, 512:768]
    qdec = qdec_ref[...]
    kdec = kdec_ref[...]
    for c in range(tb // RET_CHUNK):
        r0 = c * RET_CHUNK
        qc, kc, vc = q[r0:r0 + RET_CHUNK], k[r0:r0 + RET_CHUNK], v[r0:r0 + RET_CHUNK]
        st = state_ref[...]
        o = _dot(qc, st) * qdec
        for h in range(RET_HEADS):
            hm = (lane // RET_HD) == h
            sc = _dot(jnp.where(hm, qc, 0.0), kc, _NT) * dmask_ref[h]
            o = o + _dot(sc, jnp.where(hm, vc, 0.0))
        kv = _dot(kc * kdec, vc, _TN)
        state_ref[...] = st * cdec_ref[...] + kv * bmask_ref[...]
        o_scr[r0:r0 + RET_CHUNK, :] = o
    o = o_scr[...]
    bones = bones_ref[...]
    mu = _dot(o, bones, hi=True) * (1.0 / RET_HD)
    d = o - mu
    var = _dot(d * d, bones, hi=True) * (1.0 / RET_HD)
    on = d * lax.rsqrt(var + EPS) * gn_ref[...]
    g = x_ref[:, 768:1024]
    y = g * jax.nn.sigmoid(g) * on
    out_ref[...] = _rms(y, bn_ref[...])


def _retention(ret_in, cos, sin, consts, gn, bn):
    s = ret_in.shape[0]
    tb = min(s, TB_RET)
    dmask, qdec, kdec, cdec, bmask, bones = consts
    return pl.pallas_call(
        _ret_kernel, grid=(s // tb,),
        in_specs=[_rows(tb, 1024), _rows(tb, 128), _rows(tb, 128), _full(dmask.shape), _full(qdec.shape),
                  _full(kdec.shape), _full(cdec.shape), _full(bmask.shape), _full(bones.shape),
                  _full((1, GROUP_W)), _full((1, GROUP_W))],
        out_specs=_rows(tb, GROUP_W),
        out_shape=jax.ShapeDtypeStruct((s, GROUP_W), f32),
        scratch_shapes=[pltpu.VMEM((GROUP_W, GROUP_W), f32), pltpu.VMEM((tb, GROUP_W), f32)],
        compiler_params=_params(), name="retention")(ret_in, cos, sin, dmask, qdec, kdec, cdec, bmask, bones, gn, bn)


def _lru_kernel(x_ref, cw_ref, cb_ref, wa_ref, ba_ref, wx_ref, bx_ref, lam_ref, bn_ref, out_ref, xbuf, carry_ref,
                h_scr):
    tb = x_ref.shape[0]

    @pl.when(pl.program_id(0) == 0)
    def _():
        xbuf[0:8, :] = jnp.zeros((8, GROUP_W), f32)
        carry_ref[...] = jnp.zeros_like(carry_ref)

    x = x_ref[:, 256:512]
    xbuf[8:8 + tb, :] = x
    cw = cw_ref[...]
    xb = (cw[3:4] * x + cw[2:3] * xbuf[7:7 + tb, :] + cw[1:2] * xbuf[6:6 + tb, :]
          + cw[0:1] * xbuf[5:5 + tb, :]) + cb_ref[...]
    xbuf[0:8, :] = xbuf[tb:tb + 8, :]
    xs = _split(xb)
    r = jax.nn.sigmoid(_dot3(xs, _split(wa_ref[...])) + ba_ref[...])
    i = jax.nn.sigmoid(_dot3(xs, _split(wx_ref[...])) + bx_ref[...])
    log_a = -LRU_C * r * jax.nn.softplus(-lam_ref[...])
    a = jnp.exp(log_a)
    y2 = 2.0 * log_a
    one_minus_a2 = -jnp.tanh(0.5 * y2) * (jnp.exp(y2) + 1.0)
    b = jnp.sqrt(one_minus_a2) * (i * xb)
    row8 = lax.broadcasted_iota(jnp.int32, b.shape, 0) % S5_TILE
    d = 1
    while d < S5_TILE:
        b = a * jnp.where(row8 >= d, pltpu.roll(b, d, 0), 0.0) + b
        a = a * jnp.where(row8 >= d, pltpu.roll(a, d, 0), 1.0)
        d *= 2
    c = carry_ref[...]
    for t in range(tb // S5_TILE):
        r0 = t * S5_TILE
        ht = b[r0:r0 + S5_TILE, :] + a[r0:r0 + S5_TILE, :] * c
        h_scr[r0:r0 + S5_TILE, :] = ht
        c = ht[S5_TILE - 1:S5_TILE, :]
    carry_ref[...] = c
    y = jax.nn.gelu(x_ref[:, 0:256]) * h_scr[...]
    out_ref[...] = _rms(y, bn_ref[...])


def _rglru(lru_in, cw, cb, wa, ba, wx, bx, lam, bn):
    s = lru_in.shape[0]
    tb = min(s, TB_LRU)
    row = _full((1, GROUP_W))
    sq = _full((GROUP_W, GROUP_W))
    return pl.pallas_call(
        _lru_kernel, grid=(s // tb,),
        in_specs=[_rows(tb, 512), _full((CONV_K, GROUP_W)), row, sq, row, sq, row, row, row],
        out_specs=_rows(tb, GROUP_W),
        out_shape=jax.ShapeDtypeStruct((s, GROUP_W), f32),
        scratch_shapes=[pltpu.VMEM((tb + 8, GROUP_W), f32), pltpu.VMEM((1, GROUP_W), f32),
                        pltpu.VMEM((tb, GROUP_W), f32)],
        compiler_params=_params(), name="rglru")(lru_in, cw, cb, wa, ba, wx, bx, lam, bn)


def _s5_kernel(u_ref, a_ref, bre_ref, bim_ref, cre_ref, cim_ref, d_ref, gw_ref, gb_ref, bn_ref, out_ref, carry_ref,
               xr_scr, xi_scr):
    tb = u_ref.shape[0]

    @pl.when(pl.program_id(0) == 0)
    def _():
        carry_ref[...] = jnp.zeros_like(carry_ref)

    u = u_ref[...]
    ar = a_ref[0:1, :]
    ai = a_ref[1:2, :]
    xr = _dot(u, bre_ref[...])
    xi = _dot(u, bim_ref[...])
    row8 = lax.broadcasted_iota(jnp.int32, xr.shape, 0) % S5_TILE
    prow = lax.broadcasted_iota(jnp.int32, (S5_TILE, S5_N), 0)
    pr = jnp.broadcast_to(ar, (S5_TILE, S5_N))
    pi = jnp.broadcast_to(ai, (S5_TILE, S5_N))
    d = 1
    while d < S5_TILE:
        sr = jnp.where(row8 >= d, pltpu.roll(xr, d, 0), 0.0)
        si = jnp.where(row8 >= d, pltpu.roll(xi, d, 0), 0.0)
        xr, xi = xr + (ar * sr - ai * si), xi + (ar * si + ai * sr)
        qr = jnp.where(prow >= d, pltpu.roll(pr, d, 0), 1.0)
        qi = jnp.where(prow >= d, pltpu.roll(pi, d, 0), 0.0)
        pr, pi = pr * qr - pi * qi, pr * qi + pi * qr
        ar, ai = ar * ar - ai * ai, 2.0 * ar * ai
        d *= 2
    cr = carry_ref[0:1, :]
    ci = carry_ref[1:2, :]
    for t in range(tb // S5_TILE):
        r0 = t * S5_TILE
        tr = xr[r0:r0 + S5_TILE, :] + (pr * cr - pi * ci)
        ti = xi[r0:r0 + S5_TILE, :] + (pr * ci + pi * cr)
        xr_scr[r0:r0 + S5_TILE, :] = tr
        xi_scr[r0:r0 + S5_TILE, :] = ti
        cr = tr[S5_TILE - 1:S5_TILE, :]
        ci = ti[S5_TILE - 1:S5_TILE, :]
    carry_ref[0:1, :] = cr
    carry_ref[1:2, :] = ci
    y = _dot(xr_scr[...], cre_ref[...]) - _dot(xi_scr[...], cim_ref[...]) + d_ref[...] * u
    zz = _dot(jax.nn.gelu(y), gw_ref[...]) + gb_ref[...]
    o = zz[:, 0:GROUP_W] * jax.nn.sigmoid(zz[:, GROUP_W:2 * GROUP_W])
    out_ref[...] = _rms(o, bn_ref[...])


def _s5(s5_in, a, bre, bim, cre, cim, dd, gw, gb, bn):
    s = s5_in.shape[0]
    tb = min(s, TB_S5)
    row = _full((1, GROUP_W))
    return pl.pallas_call(
        _s5_kernel, grid=(s // tb,),
        in_specs=[_rows(tb, GROUP_W), _full((2, S5_N)), _full((GROUP_W, S5_N)), _full((GROUP_W, S5_N)),
                  _full((S5_N, GROUP_W)), _full((S5_N, GROUP_W)), row, _full((GROUP_W, 2 * GROUP_W)),
                  _full((1, 2 * GROUP_W)), row],
        out_specs=_rows(tb, GROUP_W),
        out_shape=jax.ShapeDtypeStruct((s, GROUP_W), f32),
        scratch_shapes=[pltpu.VMEM((2, S5_N), f32), pltpu.VMEM((tb, S5_N), f32), pltpu.VMEM((tb, S5_N), f32)],
        compiler_params=_params(), name="s5")(s5_in, a, bre, bim, cre, cim, dd, gw, gb, bn)


def _gdn_kernel(x_ref, ba_ref, cw_ref, alog_ref, dtb_ref, eb_ref, ea_ref, bones_ref, nw_ref, bn_ref, out_ref,
                xbuf, state_ref, o_scr):
    tb = x_ref.shape[0]
    c_ = GDN_CHUNK

    @pl.when(pl.program_id(0) == 0)
    def _():
        xbuf[0:8, :] = jnp.zeros((8, 3 * GROUP_W), f32)
        state_ref[...] = jnp.zeros_like(state_ref)

    x = x_ref[:, 0:768]
    xbuf[8:8 + tb, :] = x
    cw = cw_ref[...]
    y = cw[3:4] * x + cw[2:3] * xbuf[7:7 + tb, :] + cw[1:2] * xbuf[6:6 + tb, :] + cw[0:1] * xbuf[5:5 + tb, :]
    xbuf[0:8, :] = xbuf[tb:tb + 8, :]
    y = y * jax.nn.sigmoid(y)
    bones = bones_ref[...]
    q = y[:, 0:256]
    k = y[:, 256:512]
    v = y[:, 512:768]
    q = q * lax.rsqrt(_dot_sel(q * q, bones) + EPS) * (GDN_HD ** -0.5)
    k = k * lax.rsqrt(_dot_sel(k * k, bones) + EPS)
    ba = ba_ref[...]
    beta_b = _dot_sel(jax.nn.sigmoid(ba), eb_ref[...])
    la = -jnp.exp(alog_ref[...]) * jax.nn.softplus(ba + dtb_ref[...])
    la_b = _dot_sel(la, ea_ref[...])

    ri = lax.broadcasted_iota(jnp.int32, (c_, c_), 0)
    ci = lax.broadcasted_iota(jnp.int32, (c_, c_), 1)
    incl = ri >= ci
    strict = ri > ci
    eye = ri == ci
    ltri = incl.astype(bf16)
    ri4 = lax.broadcasted_iota(jnp.int32, (c_, GROUP_W), 0)
    ci4 = lax.broadcasted_iota(jnp.int32, (c_, GROUP_W), 1) % GDN_HD
    lev_masks = []
    b = 1
    while b < c_:
        lev_masks.append(((ri // (2 * b)) == (ci // (2 * b))) & (((ri // b) % 2) == 1) & (((ci // b) % 2) == 0))
        b *= 2

    n_chunks = tb // c_
    items = [(c, h) for c in range(n_chunks) for h in range(GDN_HEADS)]
    per_chunk = []
    for c in range(n_chunks):
        r0 = c * c_
        la_c = la_b[r0:r0 + c_, :]
        gcum = _dot_sel_lhs(ltri, la_c)
        dmat = _dot_sel_lhs(ltri, jnp.where(ri4 > ci4, la_c, 0.0))
        lm = jnp.where(ri4 >= ci4, jnp.exp(dmat), 0.0)
        eg = jnp.exp(gcum)
        gl = gcum[c_ - 1:c_, :]
        qc, kc, vc, bc = q[r0:r0 + c_, :], k[r0:r0 + c_, :], v[r0:r0 + c_, :], beta_b[r0:r0 + c_, :]
        kb = kc * bc
        per_chunk.append(dict(lm=lm, k=kc, q=qc, kb=kb, vb=vc * bc, kbe=kb * eg, qd=qc * eg,
                              kd=kc * jnp.exp(gl - gcum), cd=jnp.exp(gl)))
    lmat_f, lmat, attn = {}, {}, {}
    for (c, h) in items:
        cs = slice(h * GDN_HD, (h + 1) * GDN_HD)
        pc = per_chunk[c]
        lm = pc['lm'][:, cs]
        kh = pc['k'][:, cs]
        lmat_f[c, h] = jnp.where(strict, _dot(pc['kb'][:, cs], kh, _NT) * lm, 0.0)
        lmat[c, h] = _split(lmat_f[c, h])
        attn[c, h] = jnp.where(incl, _dot(pc['q'][:, cs], kh, _NT) * lm, 0.0)
    tinv = {n: _split(jnp.where(eye, 1.0, 0.0) - jnp.where(lev_masks[0], lmat_f[n], 0.0)) for n in items}
    for m in lev_masks[1:]:
        xs = {n: _split(_dot3((jnp.where(m, lmat[n][0], jnp.zeros_like(lmat[n][0])),
                               jnp.where(m, lmat[n][1], jnp.zeros_like(lmat[n][1]))), tinv[n])) for n in items}
        tinv = {n: _split(tinv[n][0].astype(f32) + tinv[n][1].astype(f32) - _dot3(tinv[n], xs[n])) for n in items}
    u, w, mm, nn, pp, rr = {}, {}, {}, {}, {}, {}
    for (c, h) in items:
        cs = slice(h * GDN_HD, (h + 1) * GDN_HD)
        pc = per_chunk[c]
        u[c, h] = _dot3(tinv[c, h], _split(pc['vb'][:, cs]))
        w[c, h] = _dot3(tinv[c, h], _split(pc['kbe'][:, cs]))
    for (c, h) in items:
        cs = slice(h * GDN_HD, (h + 1) * GDN_HD)
        pc = per_chunk[c]
        kd = _split(pc['kd'][:, cs])
        mm[c, h] = _split(jnp.where(eye, pc['cd'][:, cs], 0.0) - _dot3(kd, _split(w[c, h]), _TN))
        nn[c, h] = _dot3(kd, _split(u[c, h]), _TN)
        pp[c, h] = _dot(attn[c, h], u[c, h])
        rr[c, h] = _split(pc['qd'][:, cs] - _dot(attn[c, h], w[c, h]))
    for c in range(n_chunks):
        r0 = c * c_
        for h in range(GDN_HEADS):
            cs = slice(h * GDN_HD, (h + 1) * GDN_HD)
            st = _split(state_ref[h])
            o_scr[r0:r0 + c_, cs] = pp[c, h] + _dot3(rr[c, h], st)
            state_ref[h] = _dot3(mm[c, h], st) + nn[c, h]
    o = o_scr[...]
    on = o * lax.rsqrt(_dot_sel(o * o, bones) * (1.0 / GDN_HD) + EPS) * nw_ref[...]
    g = x_ref[:, 768:1024]
    yy = on * (g * jax.nn.sigmoid(g))
    out_ref[...] = _rms(yy, bn_ref[...])


def _gdn(gdn_in, ba, cw, alog, dtb, eb, ea, bones, nw, bn):
    s = gdn_in.shape[0]
    tb = min(s, TB_GDN)
    row = _full((1, GROUP_W))
    return pl.pallas_call(
        _gdn_kernel, grid=(s // tb,),
        in_specs=[_rows(tb, 1024), _rows(tb, 128), _full((CONV_K, 3 * GROUP_W)), _full((1, 128)), _full((1, 128)),
                  _full((128, GROUP_W)), _full((128, GROUP_W)), _full((GROUP_W, GROUP_W)), row, row],
        out_specs=_rows(tb, GROUP_W),
        out_shape=jax.ShapeDtypeStruct((s, GROUP_W), f32),
        scratch_shapes=[pltpu.VMEM((tb + 8, 3 * GROUP_W), f32), pltpu.VMEM((GDN_HEADS, GDN_HD, GDN_HD), f32),
                        pltpu.VMEM((tb, GROUP_W), f32)],
        compiler_params=_params(), name="gated_deltanet")(gdn_in, ba, cw, alog, dtb, eb, ea, bones, nw, bn)


def _out_kernel(h_ref, m0_ref, m1_ref, m2_ref, m3_ref, w_ref, out_ref):
    acc = h_ref[...]
    for j, m in enumerate((m0_ref, m1_ref, m2_ref, m3_ref)):
        acc = acc + jnp.dot(m[...].astype(bf16), w_ref[j * GROUP_W:(j + 1) * GROUP_W, :], preferred_element_type=f32)
    out_ref[...] = acc


def _out_proj(h, mixed, w):
    s = h.shape[0]
    tb = min(s, TB_OUT)
    return pl.pallas_call(
        _out_kernel, grid=(s // tb,),
        in_specs=[_rows(tb, D_MODEL)] + [_rows(tb, GROUP_W)] * 4 + [_full((D_MODEL, D_MODEL))],
        out_specs=_rows(tb, D_MODEL),
        out_shape=jax.ShapeDtypeStruct((s, D_MODEL), f32),
        compiler_params=_params(), name="out_proj")(h, *mixed, w)


def _extract_topk(s, idx_col, n_bad, put):
    for r in range(PEER_TOPK):
        m = jnp.max(s, axis=0, keepdims=True)
        idx = jnp.min(jnp.where(s == m, idx_col, n_bad), axis=0, keepdims=True)
        oh = idx_col == idx
        put(r, m, oh)
        s = jnp.where(oh, -jnp.inf, s)


def _extract_topk_distinct(s, put):
    for r in range(PEER_TOPK):
        m = jnp.max(s, axis=0, keepdims=True)
        oh = s == m
        put(r, m, oh)
        s = jnp.where(oh, -jnp.inf, s)


def _any_count_off(marks):
    n = jnp.sum(marks, axis=0, keepdims=True)
    return jnp.max(jnp.abs(n - float(PEER_TOPK))) > 0.5


def _topk_kernel(h_ref, nw_ref, wqt_ref, sk_ref, p1_ref, p2_ref, cidx_ref, qsel_ref,
                 zt_ref, r2_ref, jc_ref, e1_ref, e2_ref, s_scr, rank_scr, val_scr, sel_scr):
    tb = h_ref.shape[0]
    z = _rms(h_ref[...], nw_ref[...])
    zt = z.T.astype(bf16)
    zt_ref[...] = zt
    qt = jnp.dot(wqt_ref[...], zt, preferred_element_type=f32)
    for hp in range(2 * PEER_HEADS):
        s_scr[hp] = _dot(sk_ref[hp], qt[hp * PEER_HALF:(hp + 1) * PEER_HALF, :])

    key_idx = lax.broadcasted_iota(jnp.int32, (PEER_NKEYS, tb), 0).astype(f32)

    def stage1(h, carry):
        hps = (2 * h, 2 * h + 1)
        ranks = [jnp.full((PEER_NKEYS, tb), float(PEER_TOPK), f32) for _ in hps]
        ss = [s_scr[hp] for hp in hps]
        for r in range(PEER_TOPK):
            ms = [jnp.max(s, axis=0, keepdims=True) for s in ss]
            ohs = [s == m for s, m in zip(ss, ms)]
            for i, hp in enumerate(hps):
                val_scr[hp, r:r + 1, :] = ms[i]
                ranks[i] = jnp.where(ohs[i], float(r), ranks[i])
            ss = [jnp.where(oh, -jnp.inf, s) for s, oh in zip(ss, ohs)]
        for i, hp in enumerate(hps):
            rank_scr[hp] = ranks[i]
        marks = jnp.concatenate([jnp.where(rk < float(PEER_TOPK), 1.0, 0.0) for rk in ranks], axis=1)

        @pl.when(_any_count_off(marks))
        def _():
            for hp in hps:
                rank = [jnp.full((PEER_NKEYS, tb), float(PEER_TOPK), f32)]

                def put(r, m, oh, hp=hp, rank=rank):
                    val_scr[hp, r:r + 1, :] = m
                    rank[0] = jnp.where(oh, float(r), rank[0])

                _extract_topk(s_scr[hp], key_idx, float(PEER_NKEYS), put)
                rank_scr[hp] = rank[0]

        return carry

    lax.fori_loop(0, PEER_HEADS, stage1, 0)

    cidx = jnp.broadcast_to(cidx_ref[...], (PEER_NCAND, tb))
    p1 = p1_ref[...]
    p2 = p2_ref[...]

    def stage2(h, carry):
        v1 = val_scr[2 * h]
        v2 = val_scr[2 * h + 1]
        cand = _dot(p1, v1, hi=True) + _dot(p2, v2, hi=True)
        cand = jnp.where(cidx < 256.0, cand, -jnp.inf)
        def run(extract):
            sel = [jnp.zeros((PEER_NCAND, tb), f32)]

            def put(r, m, oh):
                sel[0] = jnp.where(oh, 1.0, sel[0])

            extract(cand, put)
            sel_scr[...] = sel[0]
            return sel[0]

        sel_fast = run(_extract_topk_distinct)

        @pl.when(_any_count_off(sel_fast))
        def _():
            run(lambda s, put: _extract_topk(s, cidx, 1e9, put))

        sel = sel_scr[...]
        m1 = v1[0:1, :]
        m2 = v2[0:1, :]
        wgt = sel * _dot_sel_lhs(p1.astype(bf16), jnp.exp(v1 - m1)) * _dot_sel_lhs(p2.astype(bf16), jnp.exp(v2 - m2))
        zsum = jnp.sum(wgt, axis=0, keepdims=True)
        cnt = _bdot(qsel_ref[...].astype(bf16), sel.astype(bf16))
        r1 = rank_scr[2 * h]
        r2 = rank_scr[2 * h + 1]
        jc = jnp.zeros((PEER_NKEYS, tb), f32)
        for i in range(PEER_TOPK):
            jc = jnp.where(r1 == float(i), cnt[i:i + 1, :], jc)
        r2_ref[h] = r2.astype(bf16)
        jc_ref[h] = jc
        e1_ref[h] = jnp.where(r1 < float(PEER_TOPK), jnp.exp(s_scr[2 * h] - m1), 0.0)
        e2_ref[h] = (jnp.where(r2 < float(PEER_TOPK), jnp.exp(s_scr[2 * h + 1] - m2), 0.0) / zsum).astype(bf16)
        return carry

    lax.fori_loop(0, PEER_HEADS, stage2, 0)


def _peer_topk(h, nw, wqt, sk, p1, p2, cidx, qsel):
    s = h.shape[0]
    tb = min(s, TB_TOPK)
    gate_spec = pl.BlockSpec((PEER_HEADS, PEER_NKEYS, tb), lambda i: (0, 0, i))
    gate_shape = jax.ShapeDtypeStruct((PEER_HEADS, PEER_NKEYS, s), f32)
    gate_shape16 = jax.ShapeDtypeStruct((PEER_HEADS, PEER_NKEYS, s), bf16)
    return pl.pallas_call(
        _topk_kernel, grid=(s // tb,),
        in_specs=[_rows(tb, D_MODEL), _full((1, D_MODEL)), _full((D_MODEL, D_MODEL)),
                  _full((2 * PEER_HEADS, PEER_NKEYS, PEER_HALF)), _full((PEER_NCAND, PEER_TOPK)),
                  _full((PEER_NCAND, PEER_TOPK)), _full((PEER_NCAND, 1)), _full((PEER_TOPK, PEER_NCAND))],
        out_specs=[pl.BlockSpec((D_MODEL, tb), lambda i: (0, i)), gate_spec, gate_spec, gate_spec, gate_spec],
        out_shape=[jax.ShapeDtypeStruct((D_MODEL, s), bf16), gate_shape16, gate_shape, gate_shape, gate_shape16],
        scratch_shapes=[pltpu.VMEM((2 * PEER_HEADS, PEER_NKEYS, tb), f32),
                        pltpu.VMEM((2 * PEER_HEADS, PEER_NKEYS, tb), f32),
                        pltpu.VMEM((2 * PEER_HEADS, PEER_TOPK, tb), f32), pltpu.VMEM((PEER_NCAND, tb), f32)],
        compiler_params=_params(), name="peer_topk")(h, nw, wqt, sk, p1, p2, cidx, qsel)


def _peer_kernel(zt_ref, h_ref, u_ref, vt_ref, r2_ref, jc_ref, e1_ref, e2_ref, out_ref, acc_ref, act_ref, hh_ref):
    j = pl.program_id(1)
    tb = h_ref.shape[0]
    ne = u_ref.shape[0]
    na = ne // PEER_NKEYS
    n_tiles = pl.num_programs(1) - 2

    @pl.when(j == 0)
    def _():
        acc_ref[...] = jnp.zeros_like(acc_ref)
        act_ref[...] = jnp.zeros_like(act_ref)
        hh_ref[...] = jnp.zeros_like(hh_ref)

    cur = j % 2
    prev = 1 - cur
    tile = jnp.clip(j - 1, 0, n_tiles - 1)
    n_mm = na
    rows_v = acc_ref.shape[0] // n_mm
    rows_u = ne // n_mm
    for ai in range(na):
        mi = ai * n_mm // na
        if ai * n_mm % na == 0:
            v0 = mi * rows_v
            acc_ref[v0:v0 + rows_v, :] += jnp.dot(vt_ref[v0:v0 + rows_v, :], hh_ref[cur],
                                                   preferred_element_type=f32)
        a = tile * na + ai
        g = jnp.zeros((PEER_NKEYS, tb), bf16)
        for h in range(PEER_HEADS):
            jc = jc_ref[h, pl.ds(a, 1), :].astype(bf16)
            e1 = e1_ref[h, pl.ds(a, 1), :].astype(bf16)
            g = g + jnp.where(r2_ref[h] < jc, e2_ref[h] * e1, jnp.zeros_like(g))
        r0 = ai * PEER_NKEYS
        hh_ref[prev, r0:r0 + PEER_NKEYS, :] = g * jax.nn.gelu(act_ref[prev, r0:r0 + PEER_NKEYS, :].astype(bf16))
        if (ai + 1) * n_mm % na == 0:
            u0 = mi * rows_u
            act_ref[cur, u0:u0 + rows_u, :] = jnp.dot(u_ref[u0:u0 + rows_u, :], zt_ref[...],
                                                      preferred_element_type=f32)

    @pl.when(j == pl.num_programs(1) - 1)
    def _():
        out_ref[...] = h_ref[...] + acc_ref[...].T


def _peer_dense(zt, h, u, vt, r2, jc, e1, e2):
    s = h.shape[0]
    tb = min(s, TB_PEER)
    n_tiles = u.shape[0] // NE_TILE
    gate_spec = pl.BlockSpec((PEER_HEADS, PEER_NKEYS, tb), lambda i, j: (0, 0, i))
    return pl.pallas_call(
        _peer_kernel, grid=(s // tb, n_tiles + 2),
        in_specs=[pl.BlockSpec((D_MODEL, tb), lambda i, j: (0, i)), pl.BlockSpec((tb, D_MODEL), lambda i, j: (i, 0)),
                  pl.BlockSpec((NE_TILE, D_MODEL), lambda i, j: (jnp.minimum(j, n_tiles - 1), 0)),
                  pl.BlockSpec((None, D_MODEL, NE_TILE), lambda i, j: (jnp.clip(j - 2, 0, n_tiles - 1), 0, 0)),
                  gate_spec, gate_spec, gate_spec, gate_spec],
        out_specs=pl.BlockSpec((tb, D_MODEL), lambda i, j: (i, 0)),
        out_shape=jax.ShapeDtypeStruct((s, D_MODEL), f32),
        scratch_shapes=[pltpu.VMEM((D_MODEL, tb), f32), pltpu.VMEM((2, NE_TILE, tb), f32),
                        pltpu.VMEM((2, NE_TILE, tb), bf16)],
        compiler_params=_params(2), name="peer_dense")(zt, h, u, vt, r2, jc, e1, e2)


def _ple_kernel(h_ref, p_ref, nw_ref, wg_ref, wp_ref, fw_ref, out_ref, *, final):
    x = h_ref[...]
    gate = jax.nn.sigmoid(jnp.dot(_rms(x, nw_ref[...]).astype(bf16), wg_ref[...], preferred_element_type=f32))
    y = x + jnp.dot(p_ref[...].astype(bf16), wp_ref[...], preferred_element_type=f32) * gate
    if final:
        y = _rms(y, fw_ref[...])
    out_ref[...] = y


def _ple_in_kernel(h_ref, p_ref, nw_ref, wg_ref, wp_ref, mw_ref, w_ref, h_out_ref, ret_ref, lru_ref, s5_ref, gdn_ref,
                   ba_ref):
    x = h_ref[...]
    gate = jax.nn.sigmoid(jnp.dot(_rms(x, nw_ref[...]).astype(bf16), wg_ref[...], preferred_element_type=f32))
    y = x + jnp.dot(p_ref[...].astype(bf16), wp_ref[...], preferred_element_type=f32) * gate
    h_out_ref[...] = y
    z = _rms(y, mw_ref[...]).astype(bf16)
    off = 0
    for ref in (ret_ref, lru_ref, s5_ref, gdn_ref, ba_ref):
        w = ref.shape[1]
        ref[...] = jnp.dot(z, w_ref[:, off:off + w], preferred_element_type=f32)
        off += w


def _ple_in(h, p, nw, wg, wp, mw, w):
    s = h.shape[0]
    tb = min(s, TB_IN)
    pw = p.shape[1]
    widths = (D_MODEL, 1024, 512, 256, 1024, 128)
    return pl.pallas_call(
        _ple_in_kernel, grid=(s // tb,),
        in_specs=[_rows(tb, D_MODEL), _rows(tb, pw), _full((1, D_MODEL)), _full((D_MODEL, D_MODEL)),
                  _full((pw, D_MODEL)), _full((1, D_MODEL)), _full((D_MODEL, IN_PAD))],
        out_specs=[_rows(tb, w_) for w_ in widths],
        out_shape=[jax.ShapeDtypeStruct((s, w_), f32) for w_ in widths],
        compiler_params=_params(), name="ple_in_proj")(h, p, nw, wg, wp, mw, w)


def _ple(h, p, nw, wg, wp, fw, final):
    s = h.shape[0]
    tb = min(s, TB_PLE)
    pw = p.shape[1]
    return pl.pallas_call(
        functools.partial(_ple_kernel, final=final), grid=(s // tb,),
        in_specs=[_rows(tb, D_MODEL), _rows(tb, pw), _full((1, D_MODEL)), _full((D_MODEL, D_MODEL)),
                  _full((pw, D_MODEL)), _full((1, D_MODEL))],
        out_specs=_rows(tb, D_MODEL),
        out_shape=jax.ShapeDtypeStruct((s, D_MODEL), f32),
        compiler_params=_params(), name="ple_gate")(h, p, nw, wg, wp, fw)


def _block_diag(blocks):
    n, r, c = blocks.shape
    return jnp.einsum('nrc,nm->nrmc', blocks, jnp.eye(n, dtype=blocks.dtype)).reshape(n * r, n * c)


def _retention_consts():
    log_gamma = jnp.log(1.0 - 2.0 ** (-5.0 - jnp.arange(RET_HEADS, dtype=f32)))
    idx = jnp.arange(RET_CHUNK, dtype=f32)
    diff = idx[:, None] - idx[None, :]
    causal = diff >= 0
    dmask = jnp.where(causal, jnp.exp(log_gamma[:, None, None] * jnp.where(causal, diff, 0.0)), 0.0)
    q_decay = jnp.exp(log_gamma[:, None] * (idx + 1.0))
    k_decay = jnp.exp(log_gamma[:, None] * (RET_CHUNK - 1.0 - idx))
    chunk_decay = jnp.exp(log_gamma * RET_CHUNK)
    qdec = jnp.repeat(q_decay.T, RET_HD, axis=1)
    kdec = jnp.repeat(k_decay.T, RET_HD, axis=1)
    bmask = _block_diag(jnp.ones((RET_HEADS, RET_HD, RET_HD), f32))
    cdec = _block_diag(jnp.broadcast_to(chunk_decay[:, None, None], (RET_HEADS, RET_HD, RET_HD)))
    return dmask, qdec, kdec, cdec, bmask, bmask


def _s5_discretize(a_re, a_im, b_re, b_im, c_re, c_im, log_dt):
    dt = jnp.exp(log_dt)[:, None]
    mag = jnp.exp(a_re * dt)
    ang = a_im * dt
    ab_re, ab_im = mag * jnp.cos(ang), mag * jnp.sin(ang)
    den = a_re * a_re + a_im * a_im
    p_re, p_im = ab_re - 1.0, ab_im
    f_re = (p_re * a_re + p_im * a_im) / den
    f_im = (p_im * a_re - p_re * a_im) / den
    bb_re = f_re[..., None] * b_re - f_im[..., None] * b_im
    bb_im = f_re[..., None] * b_im + f_im[..., None] * b_re
    abar = jnp.stack([ab_re.reshape(-1), ab_im.reshape(-1)])
    bre = _block_diag(jnp.swapaxes(bb_re, 1, 2))
    bim = _block_diag(jnp.swapaxes(bb_im, 1, 2))
    cre = _block_diag(jnp.swapaxes(c_re, 1, 2))
    cim = _block_diag(jnp.swapaxes(c_im, 1, 2))
    return abar, bre, bim, cre, cim


def _peer_rank_consts():
    pairs = [(i, j) for i in range(PEER_TOPK) for j in range(PEER_TOPK) if (i + 1) * (j + 1) <= PEER_TOPK]
    p1 = np.zeros((PEER_NCAND, PEER_TOPK), np.float32)
    p2 = np.zeros((PEER_NCAND, PEER_TOPK), np.float32)
    cidx = np.full((PEER_NCAND, 1), 1e9, np.float32)
    for c, (i, j) in enumerate(pairs):
        p1[c, i] = 1.0
        p2[c, j] = 1.0
        cidx[c, 0] = PEER_TOPK * i + j
    return jnp.asarray(p1), jnp.asarray(p2), jnp.asarray(cidx), jnp.asarray(p1.T)


def _row(v):
    return v.reshape(1, -1).astype(f32)


def kernel(x, p, positions, mix_norm, w_in, ret_gn, lru_conv_w, lru_conv_b, lru_w_a, lru_b_a, lru_w_x, lru_b_x, lru_lambda, s5_a_re, s5_a_im, s5_b_re, s5_b_im, s5_c_re, s5_c_im, s5_d, s5_log_dt, s5_glu_w, s5_glu_b, gdn_conv_w, gdn_a_log, gdn_dt_bias, gdn_norm, branch_norm, w_out, ffn_norm, peer_wq, peer_subkeys, peer_u, peer_v, ple_norm, ple_wg, ple_wp, final_norm):
    bsz, seq, _ = x.shape
    depth = w_in.shape[0]
    assert bsz == 1
    h = x.reshape(seq, D_MODEL)

    half = RET_HD // 2
    inv_freq = ROPE_BASE ** (-jnp.arange(half, dtype=f32) / half)
    freq = jnp.tile(inv_freq, 128 // half).reshape(1, 128)
    cos, sin = _rope_tables(positions.reshape(seq, 1).astype(f32), freq)

    ret_consts = _retention_consts()
    bones = ret_consts[5]
    p1, p2, cidx, qsel = _peer_rank_consts()
    head_rows = jnp.arange(128)[:, None]
    head_lanes = jnp.arange(GROUP_W)[None, :] // GDN_HD
    eb = (head_rows == head_lanes).astype(f32)
    ea = (head_rows == head_lanes + GDN_HEADS).astype(f32)

    def perm_w(l):
        w = w_in[l]
        return jnp.concatenate([w[:, 0:2560], w[:, 2568:2824], w[:, 2560:2568],
                                jnp.zeros((D_MODEL, IN_PAD - 2824), f32)], axis=1).astype(bf16)

    projected = _in_proj(h, _row(mix_norm[0]), perm_w(0))
    for l in range(depth):
        ret_in, lru_in, s5_in, gdn_in, ba_in = projected

        y_ret = _retention(ret_in, cos, sin, ret_consts, _row(ret_gn[l]), _row(branch_norm[l, 0]))
        y_lru = _rglru(lru_in, lru_conv_w[l], _row(lru_conv_b[l]), _block_diag(lru_w_a[l]), _row(lru_b_a[l]),
                       _block_diag(lru_w_x[l]), _row(lru_b_x[l]), _row(lru_lambda[l]), _row(branch_norm[l, 1]))
        abar, bre, bim, cre, cim = _s5_discretize(s5_a_re[l], s5_a_im[l], s5_b_re[l], s5_b_im[l],
                                                  s5_c_re[l], s5_c_im[l], s5_log_dt[l])
        y_s5 = _s5(s5_in, abar, bre, bim, cre, cim, _row(s5_d[l]), s5_glu_w[l], _row(s5_glu_b[l]),
                   _row(branch_norm[l, 2]))
        alog = jnp.zeros((1, 128), f32).at[0, GDN_HEADS:2 * GDN_HEADS].set(gdn_a_log[l])
        dtb = jnp.zeros((1, 128), f32).at[0, GDN_HEADS:2 * GDN_HEADS].set(gdn_dt_bias[l])
        y_gdn = _gdn(gdn_in, ba_in, gdn_conv_w[l], alog, dtb, eb, ea, bones,
                     _row(jnp.tile(gdn_norm[l], GDN_HEADS)), _row(branch_norm[l, 3]))

        h = _out_proj(h, (y_ret, y_lru, y_s5, y_gdn), w_out[l].astype(bf16))

        sk = peer_subkeys[l].reshape(2 * PEER_HEADS, PEER_NKEYS, PEER_HALF)
        zt, r2, jc, e1, e2 = _peer_topk(h, _row(ffn_norm[l]), peer_wq[l].T.astype(bf16), sk, p1, p2, cidx, qsel)
        vt_tiles = peer_v[l].astype(bf16).reshape(-1, NE_TILE, D_MODEL).swapaxes(1, 2)
        h = _peer_dense(zt, h, peer_u[l].astype(bf16), vt_tiles, r2, jc, e1, e2)

        if l == depth - 1:
            h = _ple(h, p[l].reshape(seq, -1), _row(ple_norm[l]), ple_wg[l].astype(bf16), ple_wp[l].astype(bf16),
                     _row(final_norm), final=True)
        else:
            h, *projected = _ple_in(h, p[l].reshape(seq, -1), _row(ple_norm[l]), ple_wg[l].astype(bf16),
                                    ple_wp[l].astype(bf16), _row(mix_norm[l + 1]), perm_w(l + 1))
    return h.reshape(bsz, seq, D_MODEL)
```
